```python
import math
import jax, jax.numpy as jnp
from jax import lax
import numpy as np

D_MODEL = 1024
BATCH = 16
SEQ = 2048
DEPTH = 1

N_ATTN_HEADS = 8
HEAD_DIM = 64
D_ATTN = N_ATTN_HEADS * HEAD_DIM
D_CONV = D_MODEL - D_ATTN
D_MIX = D_ATTN + D_CONV
D_IN_PROJ = 3 * D_ATTN + 2 * D_CONV
CONV_WIDTH = 31
D_FF = 4 * D_MODEL
DILATED_BRANCHES = ((128, 1), (512, 4), (2048, 16))
NUM_BUCKETS = 32
MAX_DISTANCE = 2048
RMS_EPS = 1e-6
LN_EPS = 1e-5
NEG_INF = -1e30

kernel_name = "hybrid_dilated_attn_conformer_conv_layer"


def rmsnorm(x, g):
    xf = x.astype(jnp.float32)
    y = xf * lax.rsqrt(jnp.mean(xf * xf, axis=-1, keepdims=True) + RMS_EPS)
    return (y * g.astype(jnp.float32)).astype(x.dtype)


def layernorm(x, g, b):
    xf = x.astype(jnp.float32)
    mu = jnp.mean(xf, axis=-1, keepdims=True)
    var = jnp.mean(jnp.square(xf - mu), axis=-1, keepdims=True)
    y = (xf - mu) * lax.rsqrt(var + LN_EPS)
    return (y * g.astype(jnp.float32) + b.astype(jnp.float32)).astype(x.dtype)


def t5_causal_bucket(distance):
    max_exact = NUM_BUCKETS // 2
    d = jnp.maximum(distance, 1).astype(jnp.float32)
    large = max_exact + (jnp.log(d / max_exact) / math.log(MAX_DISTANCE / max_exact)
                         * (NUM_BUCKETS - max_exact)).astype(jnp.int32)
    large = jnp.minimum(large, NUM_BUCKETS - 1)
    return jnp.where(distance < max_exact, distance, large)


def dilated_window_branch(q, k, v, rel_bias, window, dilation):
    B, S, H, hd = q.shape
    W = window // dilation
    L = S // dilation
    nb = -(-L // W)
    Lp = nb * W
    Bd = B * dilation

    def to_sub(t):
        return t.reshape(B, L, dilation, H, hd).transpose(0, 2, 3, 1, 4).reshape(Bd, H, L, hd)

    qs, ks, vs = to_sub(q), to_sub(k), to_sub(v)
    qb = jnp.pad(qs, ((0, 0), (0, 0), (0, Lp - L), (0, 0))).reshape(Bd, H, nb, W, hd)

    def key_blocks(t):
        t = jnp.pad(t, ((0, 0), (0, 0), (W, Lp - L), (0, 0))).reshape(Bd, H, nb + 1, W, hd)
        return jnp.concatenate([t[:, :, :-1], t[:, :, 1:]], axis=3)

    kb, vb = key_blocks(ks), key_blocks(vs)

    qi = jnp.arange(W, dtype=jnp.int32)[:, None]
    kc = jnp.arange(2 * W, dtype=jnp.int32)[None, :]
    dist = qi + W - kc
    blk = jnp.arange(nb, dtype=jnp.int32)[:, None, None]
    key_idx = blk * W + kc[None] - W
    valid = ((dist >= 0) & (dist <= W))[None] & (key_idx >= 0)
    bucket = t5_causal_bucket(jnp.clip(dist, 0, W) * dilation)
    bias = rel_bias[bucket].astype(jnp.float32).transpose(2, 0, 1)

    scale = 1.0 / math.sqrt(hd)
    s = jnp.einsum('zhnqd,zhnkd->zhnqk', qb, kb).astype(jnp.float32) * scale + bias[:, None]
    s = jnp.where(valid, s, NEG_INF)
    m = jnp.max(s, axis=-1)
    p = jnp.exp(s - m[..., None])
    l = jnp.sum(p, axis=-1)
    o = jnp.einsum('zhnqk,zhnkd->zhnqd', p, vb.astype(jnp.float32)) / l[..., None]

    o = o.reshape(Bd, H, Lp, hd)[:, :, :L].reshape(B, dilation, H, L, hd)
    o = o.transpose(0, 3, 1, 2, 4).reshape(B, S, H, hd)
    m = m.reshape(Bd, H, Lp)[:, :, :L].reshape(B, dilation, H, L).transpose(0, 3, 1, 2).reshape(B, S, H)
    l = l.reshape(Bd, H, Lp)[:, :, :L].reshape(B, dilation, H, L).transpose(0, 3, 1, 2).reshape(B, S, H)
    return o, m, l


def dilated_attention(q, k, v, rel_bias):
    outs = [dilated_window_branch(q, k, v, rel_bias, w, d) for (w, d) in DILATED_BRANCHES]
    m_all = outs[0][1]
    for _, m_i, _ in outs[1:]:
        m_all = jnp.maximum(m_all, m_i)
    num = 0.0
    den = 0.0
    for o_i, m_i, l_i in outs:
        w_i = l_i * jnp.exp(m_i - m_all)
        num = num + w_i[..., None] * o_i
        den = den + w_i
    return num / den[..., None]


def conformer_conv(a, gate, conv_w, conv_b, ln_g, ln_b):
    u = a * jax.nn.sigmoid(gate)
    y = lax.conv_general_dilated(
        u, conv_w[:, None, :], window_strides=(1,), padding=((CONV_WIDTH - 1, 0),),
        dimension_numbers=('NWC', 'WIO', 'NWC'), feature_group_count=D_CONV)
    y = y + conv_b
    y = layernorm(y, ln_g, ln_b)
    return jax.nn.silu(y)


def setup_inputs(seed: int = 0) -> dict:
    key = jax.random.key(seed)
    ks = jax.random.split(key, 14)
    f32 = jnp.float32
    x = jax.random.normal(ks[0], (BATCH, SEQ, D_MODEL), f32)
    norm1_g = 1.0 + 0.02 * jax.random.normal(ks[1], (DEPTH, D_MODEL), f32)
    w_in = jax.random.normal(ks[2], (DEPTH, D_MODEL, D_IN_PROJ), f32) * D_MODEL ** -0.5
    conv_w = jax.random.normal(ks[3], (DEPTH, CONV_WIDTH, D_CONV), f32) * CONV_WIDTH ** -0.5
    conv_b = 0.02 * jax.random.normal(ks[4], (DEPTH, D_CONV), f32)
    conv_ln_g = 1.0 + 0.02 * jax.random.normal(ks[5], (DEPTH, D_CONV), f32)
    conv_ln_b = 0.02 * jax.random.normal(ks[6], (DEPTH, D_CONV), f32)
    w_o = jax.random.normal(ks[7], (DEPTH, D_MIX, D_MODEL), f32) * D_MIX ** -0.5
    norm2_g = 1.0 + 0.02 * jax.random.normal(ks[8], (DEPTH, D_MODEL), f32)
    w_ff1 = jax.random.normal(ks[9], (DEPTH, D_MODEL, D_FF), f32) * D_MODEL ** -0.5
    w_ff2 = jax.random.normal(ks[10], (DEPTH, D_FF, D_MODEL), f32) * D_FF ** -0.5
    rel_bias = 0.5 * jax.random.normal(ks[11], (NUM_BUCKETS, N_ATTN_HEADS), f32)
    final_g = 1.0 + 0.02 * jax.random.normal(ks[12], (D_MODEL,), f32)
    return {"x": x, "norm1_g": norm1_g, "w_in": w_in, "conv_w": conv_w, "conv_b": conv_b,
            "conv_ln_g": conv_ln_g, "conv_ln_b": conv_ln_b, "w_o": w_o, "norm2_g": norm2_g,
            "w_ff1": w_ff1, "w_ff2": w_ff2, "rel_bias": rel_bias, "final_g": final_g}


def reference(x, norm1_g, w_in, conv_w, conv_b, conv_ln_g, conv_ln_b, w_o, norm2_g,
              w_ff1, w_ff2, rel_bias, final_g):
    B, S, _ = x.shape
    split_pts = [D_ATTN, 2 * D_ATTN, 3 * D_ATTN, 3 * D_ATTN + D_CONV]
    for layer in range(DEPTH):
        h = rmsnorm(x, norm1_g[layer])
        z = jnp.einsum('bsd,de->bse', h, w_in[layer])
        q, k, v, a, gate = jnp.split(z, split_pts, axis=-1)
        q = q.reshape(B, S, N_ATTN_HEADS, HEAD_DIM)
        k = k.reshape(B, S, N_ATTN_HEADS, HEAD_DIM)
        v = v.reshape(B, S, N_ATTN_HEADS, HEAD_DIM)
        attn = dilated_attention(q, k, v, rel_bias).astype(x.dtype).reshape(B, S, D_ATTN)
        conv = conformer_conv(a, gate, conv_w[layer], conv_b[layer],
                              conv_ln_g[layer], conv_ln_b[layer])
        mixed = jnp.concatenate([attn, conv], axis=-1)
        x = x + jnp.einsum('bse,ed->bsd', mixed, w_o[layer])
        h = rmsnorm(x, norm2_g[layer])
        f = jnp.square(jax.nn.relu(jnp.einsum('bsd,df->bsf', h, w_ff1[layer])))
        x = x + jnp.einsum('bsf,fd->bsd', f, w_ff2[layer])
    return rmsnorm(x, final_g)
```

```python
import functools
import math

import jax
import jax.numpy as jnp
from jax import lax
from jax.experimental import pallas as pl
from jax.experimental.pallas import tpu as pltpu

D_MODEL = 1024
N_HEADS = 8
HEAD_DIM = 64
D_ATTN = N_HEADS * HEAD_DIM
D_CONV = D_MODEL - D_ATTN
D_QKV = 3 * D_ATTN
D_IN_PROJ = D_QKV + 2 * D_CONV
CONV_WIDTH = 31
D_FF = 4 * D_MODEL
BRANCHES = ((128, 1), (512, 4), (2048, 16))
WIN = 128
NUM_BUCKETS = 32
MAX_DISTANCE = 2048
RMS_EPS = 1e-6
LN_EPS = 1e-5
NEG_INF = -1e30
LOG2E = math.log2(math.e)

LANES = 128
HEADS_PER_GROUP = LANES // HEAD_DIM
N_GROUPS = N_HEADS // HEADS_PER_GROUP
VMEM_LIMIT_BYTES = 56 * 1024 * 1024

PROJ_ROWS = 512
CONV_HALO = 32
CONV_CHUNK = 32
FFN_ROWS = 512
FF_CHUNK = 1024


def _proj_kernel(x_ref, g1_ref, win_ref, cw_ref, cb_ref, lng_ref, lnb_ref,
                 nat_ref, d4_ref, d16_ref, conv_ref, slab_ref, ubuf_ref):
    tm = PROJ_ROWS
    x = x_ref[0]
    inv = lax.rsqrt(jnp.mean(x * x, axis=-1, keepdims=True) + RMS_EPS)
    h = (x * inv * g1_ref[...]).astype(jnp.bfloat16)

    q_scale = LOG2E / math.sqrt(HEAD_DIM)
    for g in range(D_QKV // 256):
        z = jnp.dot(h, win_ref[:, g * 256:(g + 1) * 256], preferred_element_type=jnp.float32)
        if g * 256 < D_ATTN:
            z = z * q_scale
        for half in range(2):
            s = 2 * g + half
            zs = z[:, half * LANES:(half + 1) * LANES]
            slab_ref[s] = zs
            nat_ref[0, :, s * LANES:(s + 1) * LANES] = zs.astype(jnp.bfloat16)
    for s in range(D_QKV // LANES):
        cols = slice(s * LANES, (s + 1) * LANES)
        for r in range(4):
            d4_ref[0, r, :, cols] = slab_ref[s, pl.ds(r, tm // 4, stride=4), :].astype(jnp.bfloat16)
        for r in range(16):
            d16_ref[0, r, :, cols] = slab_ref[s, pl.ds(r, tm // 16, stride=16), :].astype(jnp.bfloat16)

    @pl.when(pl.program_id(1) == 0)
    def _():
        ubuf_ref[0:CONV_HALO, :] = jnp.zeros((CONV_HALO, D_CONV), jnp.float32)

    for g in range(D_CONV // 256):
        a = jnp.dot(h, win_ref[:, D_QKV + g * 256:D_QKV + (g + 1) * 256],
                    preferred_element_type=jnp.float32)
        gate = jnp.dot(h, win_ref[:, D_QKV + D_CONV + g * 256:D_QKV + D_CONV + (g + 1) * 256],
                       preferred_element_type=jnp.float32)
        ubuf_ref[CONV_HALO:CONV_HALO + tm, g * 256:(g + 1) * 256] = a * jax.nn.sigmoid(gate)

    off = CONV_HALO - (CONV_WIDTH - 1)

    for c in range(tm // CONV_CHUNK):
        base = c * CONV_CHUNK
        acc = jnp.zeros((CONV_CHUNK, D_CONV), jnp.float32) + cb_ref[...]
        for j in range(CONV_WIDTH):
            acc = acc + ubuf_ref[base + off + j:base + off + j + CONV_CHUNK, :] * cw_ref[j:j + 1, :]
        mu = jnp.mean(acc, axis=-1, keepdims=True)
        cen = acc - mu
        var = jnp.mean(cen * cen, axis=-1, keepdims=True)
        y = cen * lax.rsqrt(var + LN_EPS) * lng_ref[...] + lnb_ref[...]
        conv_ref[0, base:base + CONV_CHUNK, :] = (y * jax.nn.sigmoid(y)).astype(jnp.bfloat16)
    ubuf_ref[0:CONV_HALO, :] = ubuf_ref[tm:tm + CONV_HALO, :]


def _project(x, g1, w_in, conv_w, conv_b, ln_g, ln_b):
    B, S, _ = x.shape
    tm = PROJ_ROWS
    const = lambda b, i: (0, 0)
    return pl.pallas_call(
        _proj_kernel,
        grid=(B, S // tm),
        in_specs=[
            pl.BlockSpec((1, tm, D_MODEL), lambda b, i: (b, i, 0)),
            pl.BlockSpec((1, D_MODEL), const),
            pl.BlockSpec((D_MODEL, D_IN_PROJ), const),
            pl.BlockSpec((CONV_WIDTH, D_CONV), const),
            pl.BlockSpec((1, D_CONV), const),
            pl.BlockSpec((1, D_CONV), const),
            pl.BlockSpec((1, D_CONV), const),
        ],
        out_specs=[
            pl.BlockSpec((1, tm, D_QKV), lambda b, i: (b, i, 0)),
            pl.BlockSpec((1, 4, tm // 4, D_QKV), lambda b, i: (b, 0, i, 0)),
            pl.BlockSpec((1, 16, tm // 16, D_QKV), lambda b, i: (b, 0, i, 0)),
            pl.BlockSpec((1, tm, D_CONV), lambda b, i: (b, i, 0)),
        ],
        out_shape=[
            jax.ShapeDtypeStruct((B, S, D_QKV), jnp.bfloat16),
            jax.ShapeDtypeStruct((B, 4, S // 4, D_QKV), jnp.bfloat16),
            jax.ShapeDtypeStruct((B, 16, S // 16, D_QKV), jnp.bfloat16),
            jax.ShapeDtypeStruct((B, S, D_CONV), jnp.bfloat16),
        ],
        scratch_shapes=[
            pltpu.VMEM((D_QKV // LANES, tm, LANES), jnp.float32),
            pltpu.VMEM((CONV_HALO + tm, D_CONV), jnp.float32),
        ],
        compiler_params=pltpu.CompilerParams(
            dimension_semantics=("parallel", "arbitrary"),
            vmem_limit_bytes=VMEM_LIMIT_BYTES),
        name="proj_conv",
    )(x, g1, w_in, conv_w, conv_b, ln_g, ln_b)


def _attn_kernel(qn_ref, kn_ref, vn_ref, q4_ref, k4_ref, v4_ref, q16_ref, k16_ref, v16_ref,
                 bias_ref, out_ref, o_ref, m_ref, l_ref):
    S = out_ref.shape[1]
    lane = lax.broadcasted_iota(jnp.int32, (1, LANES), 1)
    head_lanes = [(lane >= h * HEAD_DIM) & (lane < (h + 1) * HEAD_DIM) for h in range(HEADS_PER_GROUP)]
    head_sel = [jnp.where(hl, 1.0, 0.0).astype(jnp.bfloat16) for hl in head_lanes]

    def window_block(br, q_ref, k_ref, v_ref, blk, first, dst_start, dst_stride):
        nk = WIN if first else 2 * WIN
        q0 = pl.multiple_of(blk * WIN, WIN)
        k0 = q0 if first else pl.multiple_of((blk - 1) * WIN, WIN)
        q2 = q_ref[pl.ds(q0, WIN), :]
        k2 = k_ref[pl.ds(k0, nk), :]
        v2 = v_ref[pl.ds(k0, nk), :]
        acc = None
        row_max = []
        for h in range(HEADS_PER_GROUP):
            qh = q2 * head_sel[h]
            s = lax.dot_general(qh, k2, (((1,), (1,)), ((), ())), preferred_element_type=jnp.float32)
            s = s + bias_ref[br, h, :, 2 * WIN - nk:2 * WIN]
            m = jnp.max(s, axis=-1, keepdims=True)
            p = jnp.exp2(s - m).astype(jnp.bfloat16)
            vh = v2 * head_sel[h]
            ones_h = jnp.broadcast_to(head_sel[h], (nk, LANES))
            r = jnp.dot(p, jnp.concatenate([vh, ones_h], axis=1), preferred_element_type=jnp.float32)
            acc = r if acc is None else acc + r
            row_max.append(m)
        m2 = jnp.where(head_lanes[0], row_max[0], row_max[1])
        rows = pl.ds(dst_start, WIN) if dst_stride == 1 else pl.ds(dst_start, WIN, stride=dst_stride)
        o_ref[br, rows, :] = acc[:, :LANES]
        l_ref[br, rows, :] = acc[:, LANES:]
        m_ref[br, rows, :] = m2

    srcs = ((qn_ref, kn_ref, vn_ref), (q4_ref, k4_ref, v4_ref), (q16_ref, k16_ref, v16_ref))
    for br, (_, dil) in enumerate(BRANCHES):
        q_ref, k_ref, v_ref = (r.at[0] for r in srcs[br])
        nb = S // dil // WIN

        def first_block(res, carry, br=br, dil=dil, nb=nb, refs=(q_ref, k_ref, v_ref)):
            window_block(br, *refs, res * nb, True, res, dil)
            return carry

        lax.fori_loop(0, dil, first_block, 0)
        if nb > 1:
            def later_block(i, carry, br=br, dil=dil, nb=nb, refs=(q_ref, k_ref, v_ref)):
                res = i // (nb - 1)
                n = i % (nb - 1) + 1
                window_block(br, *refs, res * nb + n, False, res + dil * WIN * n, dil)
                return carry

            lax.fori_loop(0, dil * (nb - 1), later_block, 0)

    def merge(c, carry):
        rows = pl.ds(pl.multiple_of(c * 256, 256), 256)
        ms = [m_ref[br, rows, :] for br in range(3)]
        m_all = jnp.maximum(jnp.maximum(ms[0], ms[1]), ms[2])
        num = jnp.zeros((256, LANES), jnp.float32)
        den = jnp.zeros((256, LANES), jnp.float32)
        for br in range(3):
            w = jnp.exp2(ms[br] - m_all)
            num = num + w * o_ref[br, rows, :]
            den = den + w * l_ref[br, rows, :]
        out_ref[0, rows, :] = (num / den).astype(out_ref.dtype)
        return carry

    lax.fori_loop(0, S // 256, merge, 0)


def _attention(nat, d4, d16, bias):
    B, S, _ = nat.shape
    d4 = d4.reshape(B, S, D_QKV)
    d16 = d16.reshape(B, S, D_QKV)
    qkv_specs = []
    for _ in range(3):
        for part in range(3):
            qkv_specs.append(pl.BlockSpec((1, S, LANES),
                                          lambda b, p, part=part: (b, 0, part * N_GROUPS + p)))
    return pl.pallas_call(
        _attn_kernel,
        grid=(B, N_GROUPS),
        in_specs=qkv_specs + [
            pl.BlockSpec((3, HEADS_PER_GROUP, WIN, 2 * WIN), lambda b, p: (0, p, 0, 0)),
        ],
        out_specs=pl.BlockSpec((1, S, LANES), lambda b, p: (b, 0, p)),
        out_shape=jax.ShapeDtypeStruct((B, S, D_ATTN), jnp.bfloat16),
        scratch_shapes=[pltpu.VMEM((3, S, LANES), jnp.float32)] * 3,
        compiler_params=pltpu.CompilerParams(
            dimension_semantics=("parallel", "parallel"),
            vmem_limit_bytes=VMEM_LIMIT_BYTES),
        name="dilated_attn",
    )(nat, nat, nat, d4, d4, d4, d16, d16, d16, bias)


def _t5_causal_bucket(distance):
    max_exact = NUM_BUCKETS // 2
    d = jnp.maximum(distance, 1).astype(jnp.float32)
    large = max_exact + (jnp.log(d / max_exact) / math.log(MAX_DISTANCE / max_exact)
                         * (NUM_BUCKETS - max_exact)).astype(jnp.int32)
    large = jnp.minimum(large, NUM_BUCKETS - 1)
    return jnp.where(distance < max_exact, distance, large)


def _bias_tables(rel_bias):
    qi = jnp.arange(WIN, dtype=jnp.int32)[:, None]
    kc = jnp.arange(2 * WIN, dtype=jnp.int32)[None, :]
    dist = qi + WIN - kc
    valid = (dist >= 0) & (dist <= WIN)
    tables = []
    for _, dil in BRANCHES:
        bucket = _t5_causal_bucket(jnp.clip(dist, 0, WIN) * dil)
        bias = rel_bias[bucket].astype(jnp.float32).transpose(2, 0, 1) * LOG2E
        tables.append(jnp.where(valid[None], bias, NEG_INF))
    return jnp.stack(tables)


def _ffn_kernel(x_ref, attn_ref, conv_ref, wo_ref, g2_ref, w1_ref, w2_ref, gf_ref, out_ref):
    mixed = jnp.concatenate([attn_ref[0], conv_ref[0]], axis=-1)
    x1 = x_ref[0] + jnp.dot(mixed, wo_ref[...], preferred_element_type=jnp.float32)
    inv = lax.rsqrt(jnp.mean(x1 * x1, axis=-1, keepdims=True) + RMS_EPS)
    h = (x1 * inv * g2_ref[...]).astype(jnp.bfloat16)
    acc = x1
    for c in range(D_FF // FF_CHUNK):
        cols = slice(c * FF_CHUNK, (c + 1) * FF_CHUNK)
        f = jnp.maximum(jnp.dot(h, w1_ref[:, cols], preferred_element_type=jnp.float32), 0.0)
        f = (f * f).astype(jnp.bfloat16)
        acc = acc + jnp.dot(f, w2_ref[cols, :], preferred_element_type=jnp.float32)
    inv = lax.rsqrt(jnp.mean(acc * acc, axis=-1, keepdims=True) + RMS_EPS)
    out_ref[0] = acc * inv * gf_ref[...]


def _ffn(x, attn, conv, w_o, g2, w1, w2, gf):
    B, S, _ = x.shape
    tm = FFN_ROWS
    const = lambda b, i: (0, 0)
    resident = functools.partial(pl.BlockSpec, index_map=const, pipeline_mode=pl.Buffered(1))
    return pl.pallas_call(
        _ffn_kernel,
        grid=(B, S // tm),
        in_specs=[
            pl.BlockSpec((1, tm, D_MODEL), lambda b, i: (b, i, 0)),
            pl.BlockSpec((1, tm, D_ATTN), lambda b, i: (b, i, 0)),
            pl.BlockSpec((1, tm, D_CONV), lambda b, i: (b, i, 0)),
            resident((D_MODEL, D_MODEL)),
            pl.BlockSpec((1, D_MODEL), const),
            resident((D_MODEL, D_FF)),
            resident((D_FF, D_MODEL)),
            pl.BlockSpec((1, D_MODEL), const),
        ],
        out_specs=pl.BlockSpec((1, tm, D_MODEL), lambda b, i: (b, i, 0)),
        out_shape=jax.ShapeDtypeStruct((B, S, D_MODEL), jnp.float32),
        compiler_params=pltpu.CompilerParams(
            dimension_semantics=("parallel", "parallel"),
            vmem_limit_bytes=VMEM_LIMIT_BYTES),
        name="outproj_ffn",
    )(x, attn, conv, w_o, g2, w1, w2, gf)


def kernel(x, norm1_g, w_in, conv_w, conv_b, conv_ln_g, conv_ln_b, w_o, norm2_g, w_ff1, w_ff2,
           rel_bias, final_g):
    assert norm1_g.shape[0] == 1, "single-layer problem"
    bf16 = jnp.bfloat16
    nat, d4, d16, conv = _project(
        x, norm1_g, w_in[0].astype(bf16), conv_w[0], conv_b, conv_ln_g, conv_ln_b)
    attn = _attention(nat, d4, d16, _bias_tables(rel_bias))
    return _ffn(x, attn, conv, w_o[0].astype(bf16), norm2_g, w_ff1[0].astype(bf16),
                w_ff2[0].astype(bf16), final_g.reshape(1, D_MODEL))
```

```python
import functools
import math

import jax
import jax.numpy as jnp
from jax import lax
from jax.experimental import pallas as pl
from jax.experimental.pallas import tpu as pltpu

D_MODEL = 1024
N_HEADS = 8
HEAD_DIM = 64
D_ATTN = N_HEADS * HEAD_DIM
D_CONV = D_MODEL - D_ATTN
D_QKV = 3 * D_ATTN
D_IN_PROJ = D_QKV + 2 * D_CONV
CONV_WIDTH = 31
D_FF = 4 * D_MODEL
BRANCHES = ((128, 1), (512, 4), (2048, 16))
WIN = 128
NUM_BUCKETS = 32
MAX_DISTANCE = 2048
RMS_EPS = 1e-6
LN_EPS = 1e-5
NEG_INF = -1e30
LOG2E = math.log2(math.e)

LANES = 128
HEADS_PER_GROUP = LANES // HEAD_DIM
N_GROUPS = N_HEADS // HEADS_PER_GROUP
VMEM_LIMIT_BYTES = 56 * 1024 * 1024

PROJ_ROWS = 512
CONV_HALO = 32
CONV_CHUNK = 32
ATTN_UNROLL = 5
FFN_ROWS = 512
FF_CHUNK = 1024


def _proj_kernel(x_ref, g1_ref, win_ref, cw_ref, cb_ref, lng_ref, lnb_ref,
                 nat_ref, d4_ref, d16_ref, conv_ref, slab_ref, ubuf_ref):
    tm = PROJ_ROWS
    x = x_ref[0]
    inv = lax.rsqrt(jnp.mean(x * x, axis=-1, keepdims=True) + RMS_EPS)
    h = (x * inv * g1_ref[...]).astype(jnp.bfloat16)

    n_slabs = D_CONV // LANES

    @pl.when(pl.program_id(1) == 0)
    def _():
        ubuf_ref[:, 0:CONV_HALO, :] = jnp.zeros((n_slabs, CONV_HALO, LANES), jnp.float32)

    for g in range(D_CONV // 256):
        a = jnp.dot(h, win_ref[:, D_QKV + g * 256:D_QKV + (g + 1) * 256],
                    preferred_element_type=jnp.float32)
        gate = jnp.dot(h, win_ref[:, D_QKV + D_CONV + g * 256:D_QKV + D_CONV + (g + 1) * 256],
                       preferred_element_type=jnp.float32)
        u = a * jax.nn.sigmoid(gate)
        for half in range(2):
            ubuf_ref[2 * g + half, CONV_HALO:CONV_HALO + tm, :] = u[:, half * LANES:(half + 1) * LANES]

    off = CONV_HALO - (CONV_WIDTH - 1)
    q_scale = LOG2E / math.sqrt(HEAD_DIM)

    def qkv_group(g):
        z = jnp.dot(h, win_ref[:, g * 256:(g + 1) * 256], preferred_element_type=jnp.float32)
        if g * 256 < D_ATTN:
            z = z * q_scale
        for half in range(2):
            s = 2 * g + half
            zs = z[:, half * LANES:(half + 1) * LANES]
            slab_ref[s] = zs
            nat_ref[0, :, s * LANES:(s + 1) * LANES] = zs.astype(jnp.bfloat16)
            cols = slice(s * LANES, (s + 1) * LANES)
            for r in range(4):
                d4_ref[0, r, :, cols] = slab_ref[s, pl.ds(r, tm // 4, stride=4), :].astype(jnp.bfloat16)
            for r in range(16):
                d16_ref[0, r, :, cols] = slab_ref[s, pl.ds(r, tm // 16, stride=16), :].astype(jnp.bfloat16)

    n_chunks = tm // CONV_CHUNK
    n_groups = D_QKV // 256
    groups_done = 0
    for c in range(n_chunks):
        while groups_done * n_chunks < c * n_groups:
            qkv_group(groups_done)
            groups_done += 1
        base = c * CONV_CHUNK
        slabs = []
        for s in range(n_slabs):
            cols = slice(s * LANES, (s + 1) * LANES)
            acc = jnp.zeros((CONV_CHUNK, LANES), jnp.float32) + cb_ref[:, cols]
            for j in range(CONV_WIDTH):
                tap = ubuf_ref[s, pl.ds(base + off + j, CONV_CHUNK, stride=1), :]
                acc = acc + tap * cw_ref[j:j + 1, cols]
            slabs.append(acc)
        acc = jnp.concatenate(slabs, axis=-1)
        mu = jnp.mean(acc, axis=-1, keepdims=True)
        cen = acc - mu
        var = jnp.mean(cen * cen, axis=-1, keepdims=True)
        y = cen * lax.rsqrt(var + LN_EPS) * lng_ref[...] + lnb_ref[...]
        conv_ref[0, base:base + CONV_CHUNK, :] = (y * jax.nn.sigmoid(y)).astype(jnp.bfloat16)
    for g in range(groups_done, n_groups):
        qkv_group(g)
    ubuf_ref[:, 0:CONV_HALO, :] = ubuf_ref[:, tm:tm + CONV_HALO, :]


def _project(x, g1, w_in, conv_w, conv_b, ln_g, ln_b):
    B, S, _ = x.shape
    tm = PROJ_ROWS
    const = lambda b, i: (0, 0)
    return pl.pallas_call(
        _proj_kernel,
        grid=(B, S // tm),
        in_specs=[
            pl.BlockSpec((1, tm, D_MODEL), lambda b, i: (b, i, 0)),
            pl.BlockSpec((1, D_MODEL), const),
            pl.BlockSpec((D_MODEL, D_IN_PROJ), const),
            pl.BlockSpec((CONV_WIDTH, D_CONV), const),
            pl.BlockSpec((1, D_CONV), const),
            pl.BlockSpec((1, D_CONV), const),
            pl.BlockSpec((1, D_CONV), const),
        ],
        out_specs=[
            pl.BlockSpec((1, tm, D_QKV), lambda b, i: (b, i, 0)),
            pl.BlockSpec((1, 4, tm // 4, D_QKV), lambda b, i: (b, 0, i, 0)),
            pl.BlockSpec((1, 16, tm // 16, D_QKV), lambda b, i: (b, 0, i, 0)),
            pl.BlockSpec((1, tm, D_CONV), lambda b, i: (b, i, 0)),
        ],
        out_shape=[
            jax.ShapeDtypeStruct((B, S, D_QKV), jnp.bfloat16),
            jax.ShapeDtypeStruct((B, 4, S // 4, D_QKV), jnp.bfloat16),
            jax.ShapeDtypeStruct((B, 16, S // 16, D_QKV), jnp.bfloat16),
            jax.ShapeDtypeStruct((B, S, D_CONV), jnp.bfloat16),
        ],
        scratch_shapes=[
            pltpu.VMEM((D_QKV // LANES, tm, LANES), jnp.float32),
            pltpu.VMEM((D_CONV // LANES, CONV_HALO + tm, LANES), jnp.float32),
        ],
        compiler_params=pltpu.CompilerParams(
            dimension_semantics=("parallel", "arbitrary"),
            vmem_limit_bytes=VMEM_LIMIT_BYTES),
        name="proj_conv",
    )(x, g1, w_in, conv_w, conv_b, ln_g, ln_b)


def _unroll_for(trips):
    return max(u for u in range(1, ATTN_UNROLL + 1) if trips % u == 0)


def _attn_kernel(qn_ref, kn_ref, vn_ref, q4_ref, k4_ref, v4_ref, q16_ref, k16_ref, v16_ref,
                 bias_ref, out_ref, o_ref, m_ref, l_ref):
    S = out_ref.shape[1]
    lane = lax.broadcasted_iota(jnp.int32, (1, LANES), 1)
    head_lanes = [(lane >= h * HEAD_DIM) & (lane < (h + 1) * HEAD_DIM) for h in range(HEADS_PER_GROUP)]
    head_sel = [jnp.where(hl, 1.0, 0.0).astype(jnp.bfloat16) for hl in head_lanes]

    def window_block(br, q_ref, k_ref, v_ref, blk, first, dst_start, dst_stride):
        nk = WIN if first else 2 * WIN
        q0 = pl.multiple_of(blk * WIN, WIN)
        k0 = q0 if first else pl.multiple_of((blk - 1) * WIN, WIN)
        q2 = q_ref[pl.ds(q0, WIN), :]
        k2 = k_ref[pl.ds(k0, nk), :]
        v2 = v_ref[pl.ds(k0, nk), :]
        acc = None
        row_max = []
        for h in range(HEADS_PER_GROUP):
            qh = q2 * head_sel[h]
            s = lax.dot_general(qh, k2, (((1,), (1,)), ((), ())), preferred_element_type=jnp.float32)
            s = s + bias_ref[br, h, :, 2 * WIN - nk:2 * WIN]
            m = jnp.max(s, axis=-1, keepdims=True)
            p = jnp.exp2(s - m).astype(jnp.bfloat16)
            vh = v2 * head_sel[h]
            ones_h = jnp.broadcast_to(head_sel[h], (nk, LANES))
            r = jnp.dot(p, jnp.concatenate([vh, ones_h], axis=1), preferred_element_type=jnp.float32)
            acc = r if acc is None else acc + r
            row_max.append(m)
        m2 = jnp.where(head_lanes[0], row_max[0], row_max[1])
        rows = pl.ds(dst_start, WIN) if dst_stride == 1 else pl.ds(dst_start, WIN, stride=dst_stride)
        o_ref[br, rows, :] = acc[:, :LANES]
        l_ref[br, rows, :] = acc[:, LANES:]
        m_ref[br, rows, :] = m2

    srcs = ((qn_ref, kn_ref, vn_ref), (q4_ref, k4_ref, v4_ref), (q16_ref, k16_ref, v16_ref))
    for br, (_, dil) in enumerate(BRANCHES):
        q_ref, k_ref, v_ref = (r.at[0] for r in srcs[br])
        nb = S // dil // WIN

        def first_block(res, carry, br=br, dil=dil, nb=nb, refs=(q_ref, k_ref, v_ref)):
            window_block(br, *refs, res * nb, True, res, dil)
            return carry

        lax.fori_loop(0, dil, first_block, 0, unroll=_unroll_for(dil))
        if nb > 1:
            def later_block(i, carry, br=br, dil=dil, nb=nb, refs=(q_ref, k_ref, v_ref)):
                res = i // (nb - 1)
                n = i % (nb - 1) + 1
                window_block(br, *refs, res * nb + n, False, res + dil * WIN * n, dil)
                return carry

            lax.fori_loop(0, dil * (nb - 1), later_block, 0, unroll=_unroll_for(dil * (nb - 1)))

    def merge(c, carry):
        rows = pl.ds(pl.multiple_of(c * 256, 256), 256)
        ms = [m_ref[br, rows, :] for br in range(3)]
        m_all = jnp.maximum(jnp.maximum(ms[0], ms[1]), ms[2])
        num = jnp.zeros((256, LANES), jnp.float32)
        den = jnp.zeros((256, LANES), jnp.float32)
        for br in range(3):
            w = jnp.exp2(ms[br] - m_all)
            num = num + w * o_ref[br, rows, :]
            den = den + w * l_ref[br, rows, :]
        out_ref[0, rows, :] = (num / den).astype(out_ref.dtype)
        return carry

    lax.fori_loop(0, S // 256, merge, 0)


def _attention(nat, d4, d16, bias):
    B, S, _ = nat.shape
    d4 = d4.reshape(B, S, D_QKV)
    d16 = d16.reshape(B, S, D_QKV)
    qkv_specs = []
    for _ in range(3):
        for part in range(3):
            qkv_specs.append(pl.BlockSpec((1, S, LANES),
                                          lambda b, p, part=part: (b, 0, part * N_GROUPS + p)))
    return pl.pallas_call(
        _attn_kernel,
        grid=(B, N_GROUPS),
        in_specs=qkv_specs + [
            pl.BlockSpec((3, HEADS_PER_GROUP, WIN, 2 * WIN), lambda b, p: (0, p, 0, 0)),
        ],
        out_specs=pl.BlockSpec((1, S, LANES), lambda b, p: (b, 0, p)),
        out_shape=jax.ShapeDtypeStruct((B, S, D_ATTN), jnp.bfloat16),
        scratch_shapes=[pltpu.VMEM((3, S, LANES), jnp.float32)] * 3,
        compiler_params=pltpu.CompilerParams(
            dimension_semantics=("parallel", "parallel"),
            vmem_limit_bytes=VMEM_LIMIT_BYTES),
        name="dilated_attn",
    )(nat, nat, nat, d4, d4, d4, d16, d16, d16, bias)


def _t5_causal_bucket(distance):
    max_exact = NUM_BUCKETS // 2
    d = jnp.maximum(distance, 1).astype(jnp.float32)
    large = max_exact + (jnp.log(d / max_exact) / math.log(MAX_DISTANCE / max_exact)
                         * (NUM_BUCKETS - max_exact)).astype(jnp.int32)
    large = jnp.minimum(large, NUM_BUCKETS - 1)
    return jnp.where(distance < max_exact, distance, large)


def _bias_tables(rel_bias):
    qi = jnp.arange(WIN, dtype=jnp.int32)[:, None]
    kc = jnp.arange(2 * WIN, dtype=jnp.int32)[None, :]
    dist = qi + WIN - kc
    valid = (dist >= 0) & (dist <= WIN)
    tables = []
    for _, dil in BRANCHES:
        bucket = _t5_causal_bucket(jnp.clip(dist, 0, WIN) * dil)
        bias = jnp.zeros((N_HEADS, WIN, 2 * WIN), jnp.float32)
        for b in range(NUM_BUCKETS):
            bias = jnp.where(bucket[None] == b, rel_bias[b].astype(jnp.float32)[:, None, None], bias)
        tables.append(jnp.where(valid[None], bias * LOG2E, NEG_INF))
    return jnp.stack(tables)


def _ffn_kernel(x_ref, attn_ref, conv_ref, wo_ref, g2_ref, w1_ref, w2_ref, gf_ref, out_ref):
    mixed = jnp.concatenate([attn_ref[0], conv_ref[0]], axis=-1)
    x1 = x_ref[0] + jnp.dot(mixed, wo_ref[...], preferred_element_type=jnp.float32)
    inv = lax.rsqrt(jnp.mean(x1 * x1, axis=-1, keepdims=True) + RMS_EPS)
    h = (x1 * inv * g2_ref[...]).astype(jnp.bfloat16)
    acc = x1
    for c in range(D_FF // FF_CHUNK):
        cols = slice(c * FF_CHUNK, (c + 1) * FF_CHUNK)
        f = jnp.maximum(jnp.dot(h, w1_ref[:, cols], preferred_element_type=jnp.float32), 0.0)
        f = (f * f).astype(jnp.bfloat16)
        acc = acc + jnp.dot(f, w2_ref[cols, :], preferred_element_type=jnp.float32)
    inv = lax.rsqrt(jnp.mean(acc * acc, axis=-1, keepdims=True) + RMS_EPS)
    out_ref[0] = acc * inv * gf_ref[...]


def _ffn(x, attn, conv, w_o, g2, w1, w2, gf):
    B, S, _ = x.shape
    tm = FFN_ROWS
    const = lambda b, i: (0, 0)
    resident = functools.partial(pl.BlockSpec, index_map=const, pipeline_mode=pl.Buffered(1))
    return pl.pallas_call(
        _ffn_kernel,
        grid=(B, S // tm),
        in_specs=[
            pl.BlockSpec((1, tm, D_MODEL), lambda b, i: (b, i, 0)),
            pl.BlockSpec((1, tm, D_ATTN), lambda b, i: (b, i, 0)),
            pl.BlockSpec((1, tm, D_CONV), lambda b, i: (b, i, 0)),
            resident((D_MODEL, D_MODEL)),
            pl.BlockSpec((1, D_MODEL), const),
            resident((D_MODEL, D_FF)),
            resident((D_FF, D_MODEL)),
            pl.BlockSpec((1, D_MODEL), const),
        ],
        out_specs=pl.BlockSpec((1, tm, D_MODEL), lambda b, i: (b, i, 0)),
        out_shape=jax.ShapeDtypeStruct((B, S, D_MODEL), jnp.float32),
        compiler_params=pltpu.CompilerParams(
            dimension_semantics=("parallel", "parallel"),
            vmem_limit_bytes=VMEM_LIMIT_BYTES),
        name="outproj_ffn",
    )(x, attn, conv, w_o, g2, w1, w2, gf)


def kernel(x, norm1_g, w_in, conv_w, conv_b, conv_ln_g, conv_ln_b, w_o, norm2_g, w_ff1, w_ff2,
           rel_bias, final_g):
    assert norm1_g.shape[0] == 1, "single-layer problem"
    bf16 = jnp.bfloat16
    nat, d4, d16, conv = _project(
        x, norm1_g, w_in[0].astype(bf16), conv_w[0], conv_b, conv_ln_g, conv_ln_b)
    attn = _attention(nat, d4, d16, _bias_tables(rel_bias))
    return _ffn(x, attn, conv, w_o[0].astype(bf16), norm2_g, w_ff1[0].astype(bf16),
                w_ff2[0].astype(bf16), final_g.reshape(1, D_MODEL))
```

```python
import functools
import math

import jax
import jax.numpy as jnp
from jax import lax
from jax.experimental import pallas as pl
from jax.experimental.pallas import tpu as pltpu

D_MODEL = 1024
N_HEADS = 8
HEAD_DIM = 64
D_ATTN = N_HEADS * HEAD_DIM
D_CONV = D_MODEL - D_ATTN
D_QKV = 3 * D_ATTN
D_IN_PROJ = D_QKV + 2 * D_CONV
CONV_WIDTH = 31
D_FF = 4 * D_MODEL
BRANCHES = ((128, 1), (512, 4), (2048, 16))
WIN = 128
NUM_BUCKETS = 32
MAX_DISTANCE = 2048
RMS_EPS = 1e-6
LN_EPS = 1e-5
NEG_INF = -1e30
LOG2E = math.log2(math.e)

LANES = 128
HEADS_PER_GROUP = LANES // HEAD_DIM
N_GROUPS = N_HEADS // HEADS_PER_GROUP
VMEM_LIMIT_BYTES = 56 * 1024 * 1024

PROJ_ROWS = 512
CONV_HALO = 32
CONV_CHUNK = 32
ATTN_UNROLL = 16
FFN_ROWS = 512
FF_CHUNK = 1024


def _proj_kernel(x_ref, g1_ref, win_ref, nat_ref, d4_ref, d16_ref, u_ref, slab_ref):
    tm = PROJ_ROWS
    x = x_ref[0]
    inv = lax.rsqrt(jnp.mean(x * x, axis=-1, keepdims=True) + RMS_EPS)
    h = (x * inv * g1_ref[...]).astype(jnp.bfloat16)

    for g in range(D_CONV // 256):
        a = jnp.dot(h, win_ref[:, D_QKV + g * 256:D_QKV + (g + 1) * 256],
                    preferred_element_type=jnp.float32)
        gate = jnp.dot(h, win_ref[:, D_QKV + D_CONV + g * 256:D_QKV + D_CONV + (g + 1) * 256],
                       preferred_element_type=jnp.float32)
        u = a * jax.nn.sigmoid(gate)
        for half in range(2):
            u_ref[0, 2 * g + half] = u[:, half * LANES:(half + 1) * LANES]

    q_scale = LOG2E / math.sqrt(HEAD_DIM)
    for g in range(D_QKV // 256):
        z = jnp.dot(h, win_ref[:, g * 256:(g + 1) * 256], preferred_element_type=jnp.float32)
        if g * 256 < D_ATTN:
            z = z * q_scale
        for half in range(2):
            s = 2 * g + half
            zs = z[:, half * LANES:(half + 1) * LANES]
            slab_ref[s] = zs
            nat_ref[0, :, s * LANES:(s + 1) * LANES] = zs.astype(jnp.bfloat16)
            cols = slice(s * LANES, (s + 1) * LANES)
            for r in range(4):
                d4_ref[0, r, :, cols] = slab_ref[s, pl.ds(r, tm // 4, stride=4), :].astype(jnp.bfloat16)
            for r in range(16):
                d16_ref[0, r, :, cols] = slab_ref[s, pl.ds(r, tm // 16, stride=16), :].astype(jnp.bfloat16)


def _project(x, g1, w_in):
    B, S, _ = x.shape
    tm = PROJ_ROWS
    const = lambda b, i: (0, 0)
    return pl.pallas_call(
        _proj_kernel,
        grid=(B, S // tm),
        in_specs=[
            pl.BlockSpec((1, tm, D_MODEL), lambda b, i: (b, i, 0)),
            pl.BlockSpec((1, D_MODEL), const),
            pl.BlockSpec((D_MODEL, D_IN_PROJ), const),
        ],
        out_specs=[
            pl.BlockSpec((1, tm, D_QKV), lambda b, i: (b, i, 0)),
            pl.BlockSpec((1, 4, tm // 4, D_QKV), lambda b, i: (b, 0, i, 0)),
            pl.BlockSpec((1, 16, tm // 16, D_QKV), lambda b, i: (b, 0, i, 0)),
            pl.BlockSpec((1, D_CONV // LANES, tm, LANES), lambda b, i: (b, 0, i, 0)),
        ],
        out_shape=[
            jax.ShapeDtypeStruct((B, S, D_QKV), jnp.bfloat16),
            jax.ShapeDtypeStruct((B, 4, S // 4, D_QKV), jnp.bfloat16),
            jax.ShapeDtypeStruct((B, 16, S // 16, D_QKV), jnp.bfloat16),
            jax.ShapeDtypeStruct((B, D_CONV // LANES, S, LANES), jnp.float32),
        ],
        scratch_shapes=[pltpu.VMEM((D_QKV // LANES, tm, LANES), jnp.float32)],
        compiler_params=pltpu.CompilerParams(
            dimension_semantics=("parallel", "parallel"),
            vmem_limit_bytes=VMEM_LIMIT_BYTES),
        name="in_proj",
    )(x, g1, w_in)


def _unroll_for(trips):
    return max(u for u in range(1, ATTN_UNROLL + 1) if trips % u == 0)


def _attn_kernel(qn_ref, kn_ref, vn_ref, q4_ref, k4_ref, v4_ref, q16_ref, k16_ref, v16_ref,
                 bias_ref, out_ref, o_ref, c_ref):
    assert HEADS_PER_GROUP == 2
    S = out_ref.shape[1]
    lane = lax.broadcasted_iota(jnp.int32, (1, LANES), 1)
    head_lanes = [(lane >= h * HEAD_DIM) & (lane < (h + 1) * HEAD_DIM) for h in range(HEADS_PER_GROUP)]
    head_sel = [jnp.where(hl, 1.0, 0.0).astype(jnp.bfloat16) for hl in head_lanes]

    def window_block(br, q_ref, k_ref, v_ref, blk, first, dst_start, dst_stride):
        nk = WIN if first else 2 * WIN
        q0 = pl.multiple_of(blk * WIN, WIN)
        k0 = q0 if first else pl.multiple_of((blk - 1) * WIN, WIN)
        q2 = q_ref[pl.ds(q0, WIN), :]
        k2 = k_ref[pl.ds(k0, nk), :]
        v2 = v_ref[pl.ds(k0, nk), :]
        q_both = jnp.concatenate([q2 * head_sel[h] for h in range(HEADS_PER_GROUP)], axis=0)
        s = lax.dot_general(q_both, k2, (((1,), (1,)), ((), ())), preferred_element_type=jnp.float32)
        s = s + bias_ref[br, :, :, 2 * WIN - nk:2 * WIN].reshape(HEADS_PER_GROUP * WIN, nk)
        m = jnp.max(s, axis=-1, keepdims=True)
        p = jnp.exp2(s - m).astype(jnp.bfloat16)
        v_ones = jnp.concatenate([v2, jnp.ones((nk, LANES), jnp.bfloat16)], axis=1)
        r = jnp.dot(p, v_ones, preferred_element_type=jnp.float32)
        pv = jnp.where(head_lanes[0], r[:WIN, :LANES], r[WIN:, :LANES])
        l2 = jnp.where(head_lanes[0], r[:WIN, LANES:], r[WIN:, LANES:])
        m2 = jnp.where(head_lanes[0], m[:WIN], m[WIN:])
        rows = pl.ds(dst_start, WIN) if dst_stride == 1 else pl.ds(dst_start, WIN, stride=dst_stride)
        o_ref[br, rows, :] = pv / l2
        c_ref[br, rows, :] = m2 + jnp.log2(l2)

    srcs = ((qn_ref, kn_ref, vn_ref), (q4_ref, k4_ref, v4_ref), (q16_ref, k16_ref, v16_ref))
    for br, (_, dil) in enumerate(BRANCHES):
        q_ref, k_ref, v_ref = (r.at[0] for r in srcs[br])
        nb = S // dil // WIN

        def first_block(res, carry, br=br, dil=dil, nb=nb, refs=(q_ref, k_ref, v_ref)):
            window_block(br, *refs, res * nb, True, res, dil)
            return carry

        lax.fori_loop(0, dil, first_block, 0, unroll=_unroll_for(dil))
        if nb > 1:
            def later_block(i, carry, br=br, dil=dil, nb=nb, refs=(q_ref, k_ref, v_ref)):
                res = i // (nb - 1)
                n = i % (nb - 1) + 1
                window_block(br, *refs, res * nb + n, False, res + dil * WIN * n, dil)
                return carry

            lax.fori_loop(0, dil * (nb - 1), later_block, 0, unroll=_unroll_for(dil * (nb - 1)))

    def merge(c, carry):
        rows = pl.ds(pl.multiple_of(c * 256, 256), 256)
        cs = [c_ref[br, rows, :] for br in range(3)]
        c_all = jnp.maximum(jnp.maximum(cs[0], cs[1]), cs[2])
        ws = [jnp.exp2(c - c_all) for c in cs]
        num = ws[0] * o_ref[0, rows, :] + ws[1] * o_ref[1, rows, :] + ws[2] * o_ref[2, rows, :]
        out_ref[0, rows, :] = (num / (ws[0] + ws[1] + ws[2])).astype(out_ref.dtype)
        return carry

    lax.fori_loop(0, S // 256, merge, 0)


def _attention(nat, d4, d16, bias):
    B, S, _ = nat.shape
    d4 = d4.reshape(B, S, D_QKV)
    d16 = d16.reshape(B, S, D_QKV)
    qkv_specs = []
    for _ in range(3):
        for part in range(3):
            qkv_specs.append(pl.BlockSpec((1, S, LANES),
                                          lambda b, p, part=part: (b, 0, part * N_GROUPS + p)))
    return pl.pallas_call(
        _attn_kernel,
        grid=(B, N_GROUPS),
        in_specs=qkv_specs + [
            pl.BlockSpec((3, HEADS_PER_GROUP, WIN, 2 * WIN), lambda b, p: (0, p, 0, 0)),
        ],
        out_specs=pl.BlockSpec((1, S, LANES), lambda b, p: (b, 0, p)),
        out_shape=jax.ShapeDtypeStruct((B, S, D_ATTN), jnp.bfloat16),
        scratch_shapes=[pltpu.VMEM((3, S, LANES), jnp.float32)] * 2,
        compiler_params=pltpu.CompilerParams(
            dimension_semantics=("parallel", "parallel"),
            vmem_limit_bytes=VMEM_LIMIT_BYTES),
        name="dilated_attn",
    )(nat, nat, nat, d4, d4, d4, d16, d16, d16, bias)


def _t5_causal_bucket(distance):
    max_exact = NUM_BUCKETS // 2
    d = jnp.maximum(distance, 1).astype(jnp.float32)
    large = max_exact + (jnp.log(d / max_exact) / math.log(MAX_DISTANCE / max_exact)
                         * (NUM_BUCKETS - max_exact)).astype(jnp.int32)
    large = jnp.minimum(large, NUM_BUCKETS - 1)
    return jnp.where(distance < max_exact, distance, large)


def _bias_tables(rel_bias):
    qi = jnp.arange(WIN, dtype=jnp.int32)[:, None]
    kc = jnp.arange(2 * WIN, dtype=jnp.int32)[None, :]
    dist = qi + WIN - kc
    valid = (dist >= 0) & (dist <= WIN)
    tables = []
    for _, dil in BRANCHES:
        bucket = _t5_causal_bucket(jnp.clip(dist, 0, WIN) * dil)
        bias = jnp.zeros((N_HEADS, WIN, 2 * WIN), jnp.float32)
        for b in range(NUM_BUCKETS):
            bias = jnp.where(bucket[None] == b, rel_bias[b].astype(jnp.float32)[:, None, None], bias)
        tables.append(jnp.where(valid[None], bias * LOG2E, NEG_INF))
    return jnp.stack(tables)


def _ffn_kernel(x_ref, attn_ref, u_ref, uh_ref, cw_ref, cb_ref, lng_ref, lnb_ref,
                wo_ref, g2_ref, w1_ref, w2_ref, gf_ref, out_ref, ubuf_ref, conv_ref, *, tiles_per_seq):
    tm = FFN_ROWS
    step = pl.program_id(0)
    n_slabs = D_CONV // LANES

    @pl.when(step == 0)
    def _():
        conv_ref[...] = jnp.zeros_like(conv_ref)

    mixed = jnp.concatenate([attn_ref[0], conv_ref[...]], axis=-1)
    x1 = x_ref[0] + jnp.dot(mixed, wo_ref[...], preferred_element_type=jnp.float32)
    inv = lax.rsqrt(jnp.mean(x1 * x1, axis=-1, keepdims=True) + RMS_EPS)
    h = (x1 * inv * g2_ref[...]).astype(jnp.bfloat16)

    seq_start = (step % tiles_per_seq) == 0
    ubuf_ref[:, 0:CONV_HALO, :] = jnp.where(seq_start, 0.0, uh_ref[0])
    ubuf_ref[:, CONV_HALO:CONV_HALO + tm, :] = u_ref[0]
    off = CONV_HALO - (CONV_WIDTH - 1)

    def conv_chunk(c):
        base = c * CONV_CHUNK
        slabs = []
        for s in range(n_slabs):
            cols = slice(s * LANES, (s + 1) * LANES)
            a = jnp.zeros((CONV_CHUNK, LANES), jnp.float32) + cb_ref[:, cols]
            for j in range(CONV_WIDTH):
                tap = ubuf_ref[s, pl.ds(base + off + j, CONV_CHUNK, stride=1), :]
                a = a + tap * cw_ref[j:j + 1, cols]
            slabs.append(a)
        y = jnp.concatenate(slabs, axis=-1)
        mu = jnp.mean(y, axis=-1, keepdims=True)
        cen = y - mu
        var = jnp.mean(cen * cen, axis=-1, keepdims=True)
        y = cen * lax.rsqrt(var + LN_EPS) * lng_ref[...] + lnb_ref[...]
        y = y * jax.nn.sigmoid(y)
        conv_ref[base:base + CONV_CHUNK, :] = y.astype(jnp.bfloat16)
        fold = y[0:8, 0:LANES]
        for r in range(CONV_CHUNK // 8):
            for s in range(n_slabs):
                if r or s:
                    fold = jnp.maximum(fold, y[8 * r:8 * r + 8, s * LANES:(s + 1) * LANES])
        half_word = jnp.uint32(16)
        bits = lax.shift_right_logical(pltpu.bitcast(fold, jnp.uint32), half_word)
        return pltpu.bitcast(lax.shift_right_logical(bits, half_word), jnp.float32)

    n_ff = D_FF // FF_CHUNK
    chunk_ids = iter(range(tm // CONV_CHUNK))
    assert tm // CONV_CHUNK == n_ff * (FF_CHUNK // 256)

    def anchored(d):
        top = []
        for g in range(d.shape[1] // 256):
            top.append(d[0:8, 256 * g:256 * g + LANES] + conv_chunk(next(chunk_ids)))
            top.append(d[0:8, 256 * g + LANES:256 * (g + 1)])
        return jnp.concatenate([jnp.concatenate(top, axis=1), d[8:]], axis=0)

    acc = x1
    for c in range(n_ff):
        cols = slice(c * FF_CHUNK, (c + 1) * FF_CHUNK)
        f = jnp.maximum(anchored(jnp.dot(h, w1_ref[:, cols], preferred_element_type=jnp.float32)), 0.0)
        f = (f * f).astype(jnp.bfloat16)
        acc = acc + jnp.dot(f, w2_ref[cols, :], preferred_element_type=jnp.float32)
    inv = lax.rsqrt(jnp.mean(acc * acc, axis=-1, keepdims=True) + RMS_EPS)
    out_ref[0] = acc * inv * gf_ref[...]


def _ffn(x, attn, u, conv_w, conv_b, ln_g, ln_b, w_o, g2, w1, w2, gf):
    B, S, _ = x.shape
    tm = FFN_ROWS
    tiles_per_seq = S // tm
    n_tiles = B * tiles_per_seq
    halo_blocks = tm // CONV_HALO
    x = x.reshape(n_tiles, tm, D_MODEL)
    attn = attn.reshape(n_tiles, tm, D_ATTN)

    def ffn_tile(s):
        return (jnp.maximum(s - 1, 0), 0, 0)

    def conv_tile(s):
        t = jnp.minimum(s, n_tiles - 1)
        return t // tiles_per_seq, t % tiles_per_seq

    def u_block(s):
        b, i = conv_tile(s)
        return (b, 0, i, 0)

    def halo_block(s):
        b, i = conv_tile(s)
        return (b, 0, jnp.maximum(i * halo_blocks - 1, 0), 0)

    const = lambda s: (0, 0)
    resident = functools.partial(pl.BlockSpec, index_map=const, pipeline_mode=pl.Buffered(1))
    out = pl.pallas_call(
        functools.partial(_ffn_kernel, tiles_per_seq=tiles_per_seq),
        grid=(n_tiles + 1,),
        in_specs=[
            pl.BlockSpec((1, tm, D_MODEL), ffn_tile),
            pl.BlockSpec((1, tm, D_ATTN), ffn_tile),
            pl.BlockSpec((1, D_CONV // LANES, tm, LANES), u_block),
            pl.BlockSpec((1, D_CONV // LANES, CONV_HALO, LANES), halo_block),
            pl.BlockSpec((CONV_WIDTH, D_CONV), const),
            pl.BlockSpec((1, D_CONV), const),
            pl.BlockSpec((1, D_CONV), const),
            pl.BlockSpec((1, D_CONV), const),
            resident((D_MODEL, D_MODEL)),
            pl.BlockSpec((1, D_MODEL), const),
            resident((D_MODEL, D_FF)),
            resident((D_FF, D_MODEL)),
            pl.BlockSpec((1, D_MODEL), const),
        ],
        out_specs=pl.BlockSpec((1, tm, D_MODEL), ffn_tile),
        out_shape=jax.ShapeDtypeStruct((n_tiles, tm, D_MODEL), jnp.float32),
        scratch_shapes=[
            pltpu.VMEM((D_CONV // LANES, CONV_HALO + tm, LANES), jnp.float32),
            pltpu.VMEM((tm, D_CONV), jnp.bfloat16),
        ],
        compiler_params=pltpu.CompilerParams(
            dimension_semantics=("arbitrary",),
            vmem_limit_bytes=VMEM_LIMIT_BYTES),
        name="conv_outproj_ffn",
    )(x, attn, u, u, conv_w, conv_b, ln_g, ln_b, w_o, g2, w1, w2, gf)
    return out.reshape(B, S, D_MODEL)


def kernel(x, norm1_g, w_in, conv_w, conv_b, conv_ln_g, conv_ln_b, w_o, norm2_g, w_ff1, w_ff2,
           rel_bias, final_g):
    assert norm1_g.shape[0] == 1, "single-layer problem"
    bf16 = jnp.bfloat16
    nat, d4, d16, u = _project(x, norm1_g, w_in[0].astype(bf16))
    attn = _attention(nat, d4, d16, _bias_tables(rel_bias))
    return _ffn(x, attn, u, conv_w[0], conv_b, conv_ln_g, conv_ln_b, w_o[0].astype(bf16), norm2_g,
                w_ff1[0].astype(bf16), w_ff2[0].astype(bf16), final_g.reshape(1, D_MODEL))
```

```python
import functools
import math

import jax
import jax.numpy as jnp
from jax import lax
from jax.experimental import pallas as pl
from jax.experimental.pallas import tpu as pltpu

D_MODEL = 1024
N_HEADS = 8
HEAD_DIM = 64
D_ATTN = N_HEADS * HEAD_DIM
D_CONV = D_MODEL - D_ATTN
D_QKV = 3 * D_ATTN
D_IN_PROJ = D_QKV + 2 * D_CONV
CONV_WIDTH = 31
D_FF = 4 * D_MODEL
BRANCHES = ((128, 1), (512, 4), (2048, 16))
WIN = 128
NUM_BUCKETS = 32
MAX_DISTANCE = 2048
RMS_EPS = 1e-6
LN_EPS = 1e-5
NEG_INF = -1e30
LOG2E = math.log2(math.e)

LANES = 128
HEADS_PER_GROUP = LANES // HEAD_DIM
N_GROUPS = N_HEADS // HEADS_PER_GROUP
VMEM_LIMIT_BYTES = 56 * 1024 * 1024

PROJ_ROWS = 512
CONV_HALO = 32
CONV_CHUNK = 32
RESIDUE_PITCH = 136
MERGE_ROWS = 256
FFN_ROWS = 512
FF_CHUNK = 512


def _proj_kernel(x_ref, g1_ref, win_ref, nat_ref, d4_ref, d16_ref, u_ref, slab_ref):
    tm = PROJ_ROWS
    x = x_ref[0]
    inv = lax.rsqrt(jnp.mean(x * x, axis=-1, keepdims=True) + RMS_EPS)
    h = (x * inv * g1_ref[...]).astype(jnp.bfloat16)

    for g in range(D_CONV // 256):
        a = jnp.dot(h, win_ref[:, D_QKV + g * 256:D_QKV + (g + 1) * 256],
                    preferred_element_type=jnp.float32)
        gate = jnp.dot(h, win_ref[:, D_QKV + D_CONV + g * 256:D_QKV + D_CONV + (g + 1) * 256],
                       preferred_element_type=jnp.float32)
        u = a * jax.nn.sigmoid(gate)
        for half in range(2):
            u_ref[0, 2 * g + half] = u[:, half * LANES:(half + 1) * LANES]

    q_scale = LOG2E / math.sqrt(HEAD_DIM)
    for g in range(D_QKV // 256):
        z = jnp.dot(h, win_ref[:, g * 256:(g + 1) * 256], preferred_element_type=jnp.float32)
        if g * 256 < D_ATTN:
            z = z * q_scale
        for half in range(2):
            s = 2 * g + half
            zs = z[:, half * LANES:(half + 1) * LANES]
            slab_ref[s] = zs
            nat_ref[0, :, s * LANES:(s + 1) * LANES] = zs.astype(jnp.bfloat16)
            cols = slice(s * LANES, (s + 1) * LANES)
            for r in range(4):
                d4_ref[0, r, :, cols] = slab_ref[s, pl.ds(r, tm // 4, stride=4), :].astype(jnp.bfloat16)
            for r in range(16):
                d16_ref[0, r, :, cols] = slab_ref[s, pl.ds(r, tm // 16, stride=16), :].astype(jnp.bfloat16)


def _project(x, g1, w_in):
    B, S, _ = x.shape
    tm = PROJ_ROWS
    const = lambda b, i: (0, 0)
    return pl.pallas_call(
        _proj_kernel,
        grid=(B, S // tm),
        in_specs=[
            pl.BlockSpec((1, tm, D_MODEL), lambda b, i: (b, i, 0)),
            pl.BlockSpec((1, D_MODEL), const),
            pl.BlockSpec((D_MODEL, D_IN_PROJ), const),
        ],
        out_specs=[
            pl.BlockSpec((1, tm, D_QKV), lambda b, i: (b, i, 0)),
            pl.BlockSpec((1, 4, tm // 4, D_QKV), lambda b, i: (b, 0, i, 0)),
            pl.BlockSpec((1, 16, tm // 16, D_QKV), lambda b, i: (b, 0, i, 0)),
            pl.BlockSpec((1, D_CONV // LANES, tm, LANES), lambda b, i: (b, 0, i, 0)),
        ],
        out_shape=[
            jax.ShapeDtypeStruct((B, S, D_QKV), jnp.bfloat16),
            jax.ShapeDtypeStruct((B, 4, S // 4, D_QKV), jnp.bfloat16),
            jax.ShapeDtypeStruct((B, 16, S // 16, D_QKV), jnp.bfloat16),
            jax.ShapeDtypeStruct((B, D_CONV // LANES, S, LANES), jnp.float32),
        ],
        scratch_shapes=[pltpu.VMEM((D_QKV // LANES, tm, LANES), jnp.float32)],
        compiler_params=pltpu.CompilerParams(
            dimension_semantics=("parallel", "parallel"),
            vmem_limit_bytes=VMEM_LIMIT_BYTES),
        name="in_proj",
    )(x, g1, w_in)


def _attn_kernel(qn_ref, kn_ref, vn_ref, q4_ref, k4_ref, v4_ref, q16_ref, k16_ref, v16_ref,
                 bias_ref, out_ref, o_ref, c_ref):
    assert HEADS_PER_GROUP == 2
    S = out_ref.shape[1]
    lane = lax.broadcasted_iota(jnp.int32, (1, LANES), 1)
    head_lanes = [(lane >= h * HEAD_DIM) & (lane < (h + 1) * HEAD_DIM) for h in range(HEADS_PER_GROUP)]
    head_sel = [jnp.where(hl, 1.0, 0.0).astype(jnp.bfloat16) for hl in head_lanes]

    def window_block(br, q_ref, k_ref, v_ref, blk, first, dst_start, dst_stride):
        nk = WIN if first else 2 * WIN
        q0 = blk * WIN
        k0 = q0 if first else q0 - WIN
        q2 = q_ref[pl.ds(q0, WIN), :]
        k2 = k_ref[pl.ds(k0, nk), :]
        v2 = v_ref[pl.ds(k0, nk), :]
        q_both = jnp.concatenate([q2 * head_sel[h] for h in range(HEADS_PER_GROUP)], axis=0)
        s = lax.dot_general(q_both, k2, (((1,), (1,)), ((), ())), preferred_element_type=jnp.float32)
        s = s + bias_ref[br, :, :, 2 * WIN - nk:2 * WIN].reshape(HEADS_PER_GROUP * WIN, nk)
        m = jnp.max(s, axis=-1, keepdims=True)
        p = jnp.exp2(s - m).astype(jnp.bfloat16)
        p_cat = jnp.concatenate([p[h * WIN:(h + 1) * WIN] for h in range(HEADS_PER_GROUP)], axis=1)
        v_cat = jnp.concatenate(
            [jnp.concatenate([v2 * head_sel[h], jnp.broadcast_to(head_sel[h], (nk, LANES))], axis=1)
             for h in range(HEADS_PER_GROUP)], axis=0)
        r = jnp.dot(p_cat, v_cat, preferred_element_type=jnp.float32)
        pv = r[:, :LANES]
        l2 = r[:, LANES:]
        m2 = jnp.where(head_lanes[0], m[:WIN], m[WIN:])
        o_ref[br, rows(dst_start, dst_stride), :] = pv / l2
        c_ref[br, rows(dst_start, dst_stride), :] = m2 + jnp.log2(l2)

    def rows(start, stride):
        return pl.ds(start, WIN) if stride == 1 else pl.ds(start, WIN, stride=stride)

    srcs = ((qn_ref, kn_ref, vn_ref), (q4_ref, k4_ref, v4_ref), (q16_ref, k16_ref, v16_ref))
    gathered = []
    for br, (_, dil) in enumerate(BRANCHES):
        q_ref, k_ref, v_ref = (r.at[0] for r in srcs[br])
        nb = S // dil // WIN
        gathered.append(dil % 8 == 0)
        assert not gathered[br] or nb == 1
        for first in (True, False):
            for res in range(dil):
                for n in range(0 if first else 1, 1 if first else nb):
                    dst = (res * RESIDUE_PITCH, 1) if gathered[br] else (res + dil * WIN * n, dil)
                    window_block(br, q_ref, k_ref, v_ref, res * nb + n, first, *dst)

    def merge(c, carry):
        def branch_rows(ref, br):
            if not gathered[br]:
                return ref[br, pl.ds(pl.multiple_of(c * MERGE_ROWS, MERGE_ROWS), MERGE_ROWS), :]
            dil = BRANCHES[br][1]
            j0 = c * (MERGE_ROWS // dil)
            return jnp.concatenate(
                [ref[br, pl.ds(j0 + i, dil, stride=RESIDUE_PITCH), :] for i in range(MERGE_ROWS // dil)],
                axis=0)

        cs = [branch_rows(c_ref, br) for br in range(3)]
        c_all = jnp.maximum(jnp.maximum(cs[0], cs[1]), cs[2])
        ws = [jnp.exp2(cb - c_all) for cb in cs]
        num = sum(ws[br] * branch_rows(o_ref, br) for br in range(3))
        out_rows = pl.ds(pl.multiple_of(c * MERGE_ROWS, MERGE_ROWS), MERGE_ROWS)
        out_ref[0, out_rows, :] = (num / (ws[0] + ws[1] + ws[2])).astype(out_ref.dtype)
        return carry

    lax.fori_loop(0, S // MERGE_ROWS, merge, 0)


def _attention(nat, d4, d16, bias):
    B, S, _ = nat.shape
    d4 = d4.reshape(B, S, D_QKV)
    d16 = d16.reshape(B, S, D_QKV)
    qkv_specs = []
    for _ in range(3):
        for part in range(3):
            qkv_specs.append(pl.BlockSpec((1, S, LANES),
                                          lambda b, p, part=part: (b, 0, part * N_GROUPS + p)))
    return pl.pallas_call(
        _attn_kernel,
        grid=(B, N_GROUPS),
        in_specs=qkv_specs + [
            pl.BlockSpec((3, HEADS_PER_GROUP, WIN, 2 * WIN), lambda b, p: (0, p, 0, 0)),
        ],
        out_specs=pl.BlockSpec((1, S, LANES), lambda b, p: (b, 0, p)),
        out_shape=jax.ShapeDtypeStruct((B, S, D_ATTN), jnp.bfloat16),
        scratch_shapes=[pltpu.VMEM((3, max(S, 16 * RESIDUE_PITCH), LANES), jnp.float32)] * 2,
        compiler_params=pltpu.CompilerParams(
            dimension_semantics=("parallel", "parallel"),
            vmem_limit_bytes=VMEM_LIMIT_BYTES),
        name="dilated_attn",
    )(nat, nat, nat, d4, d4, d4, d16, d16, d16, bias)


def _t5_causal_bucket(distance):
    max_exact = NUM_BUCKETS // 2
    d = jnp.maximum(distance, 1).astype(jnp.float32)
    large = max_exact + (jnp.log(d / max_exact) / math.log(MAX_DISTANCE / max_exact)
                         * (NUM_BUCKETS - max_exact)).astype(jnp.int32)
    large = jnp.minimum(large, NUM_BUCKETS - 1)
    return jnp.where(distance < max_exact, distance, large)


def _bias_tables(rel_bias):
    qi = jnp.arange(WIN, dtype=jnp.int32)[:, None]
    kc = jnp.arange(2 * WIN, dtype=jnp.int32)[None, :]
    dist = qi + WIN - kc
    valid = (dist >= 0) & (dist <= WIN)
    tables = []
    for _, dil in BRANCHES:
        bucket = _t5_causal_bucket(jnp.clip(dist, 0, WIN) * dil)
        bias = jnp.zeros((N_HEADS, WIN, 2 * WIN), jnp.float32)
        for b in range(NUM_BUCKETS):
            bias = jnp.where(bucket[None] == b, rel_bias[b].astype(jnp.float32)[:, None, None], bias)
        tables.append(jnp.where(valid[None], bias * LOG2E, NEG_INF))
    return jnp.stack(tables)


def _ffn_kernel(x_ref, attn_ref, u_ref, uh_ref, cw_ref, cb_ref, lng_ref, lnb_ref,
                wo_ref, g2_ref, w1_ref, w2_ref, gf_ref, out_ref, ubuf_ref, conv_ref, *, tiles_per_seq):
    tm = FFN_ROWS
    step = pl.program_id(0)
    n_slabs = D_CONV // LANES

    @pl.when(step == 0)
    def _():
        conv_ref[...] = jnp.zeros_like(conv_ref)

    mixed = jnp.concatenate([attn_ref[0], conv_ref[...]], axis=-1)
    x1 = x_ref[0] + jnp.dot(mixed, wo_ref[...], preferred_element_type=jnp.float32)
    inv = lax.rsqrt(jnp.mean(x1 * x1, axis=-1, keepdims=True) + RMS_EPS)
    h = (x1 * inv * g2_ref[...]).astype(jnp.bfloat16)

    seq_start = (step % tiles_per_seq) == 0
    ubuf_ref[:, 0:CONV_HALO, :] = jnp.where(seq_start, 0.0, uh_ref[0])
    ubuf_ref[:, CONV_HALO:CONV_HALO + tm, :] = u_ref[0]
    off = CONV_HALO - (CONV_WIDTH - 1)

    def exact_zero(tile):
        half_word = jnp.uint32(16)
        bits = lax.shift_right_logical(pltpu.bitcast(tile, jnp.uint32), half_word)
        return pltpu.bitcast(lax.shift_right_logical(bits, half_word), jnp.float32)

    def conv_chunk(c):
        base = c * CONV_CHUNK
        slabs = []
        for s in range(n_slabs):
            cols = slice(s * LANES, (s + 1) * LANES)
            a = jnp.zeros((CONV_CHUNK, LANES), jnp.float32) + cb_ref[:, cols]
            for j in range(CONV_WIDTH):
                tap = ubuf_ref[s, pl.ds(base + off + j, CONV_CHUNK, stride=1), :]
                a = a + tap * cw_ref[j:j + 1, cols]
            slabs.append(a)
        y = jnp.concatenate(slabs, axis=-1)
        mu = jnp.mean(y, axis=-1, keepdims=True)
        cen = y - mu
        var = jnp.mean(cen * cen, axis=-1, keepdims=True)
        y = cen * lax.rsqrt(var + LN_EPS) * lng_ref[...] + lnb_ref[...]
        y = y * jax.nn.sigmoid(y)
        conv_ref[base:base + CONV_CHUNK, :] = y.astype(jnp.bfloat16)
        fold = y[0:8, 0:LANES]
        for r in range(CONV_CHUNK // 8):
            for s in range(n_slabs):
                if r or s:
                    fold = jnp.maximum(fold, y[8 * r:8 * r + 8, s * LANES:(s + 1) * LANES])
        return exact_zero(fold)

    n_ff = D_FF // FF_CHUNK
    chunk_ids = iter(range(tm // CONV_CHUNK))
    anchor_cols = FF_CHUNK * n_ff // (tm // CONV_CHUNK)
    assert anchor_cols % LANES == 0

    def anchored(d):
        top = []
        for g in range(d.shape[1] // anchor_cols):
            top.append(d[0:8, anchor_cols * g:anchor_cols * g + LANES] + conv_chunk(next(chunk_ids)))
            if anchor_cols > LANES:
                top.append(d[0:8, anchor_cols * g + LANES:anchor_cols * (g + 1)])
        return jnp.concatenate([jnp.concatenate(top, axis=1), d[8:]], axis=0)

    acc = x1
    for c in range(n_ff):
        cols = slice(c * FF_CHUNK, (c + 1) * FF_CHUNK)
        f = jnp.maximum(anchored(jnp.dot(h, w1_ref[:, cols], preferred_element_type=jnp.float32)), 0.0)
        f = (f * f).astype(jnp.bfloat16)
        acc = acc + jnp.dot(f, w2_ref[cols, :], preferred_element_type=jnp.float32)
    inv = lax.rsqrt(jnp.mean(acc * acc, axis=-1, keepdims=True) + RMS_EPS)
    out_ref[0] = acc * inv * gf_ref[...]


def _ffn(x, attn, u, conv_w, conv_b, ln_g, ln_b, w_o, g2, w1, w2, gf):
    B, S, _ = x.shape
    tm = FFN_ROWS
    tiles_per_seq = S // tm
    n_tiles = B * tiles_per_seq
    halo_blocks = tm // CONV_HALO
    x = x.reshape(n_tiles, tm, D_MODEL)
    attn = attn.reshape(n_tiles, tm, D_ATTN)

    def ffn_tile(s):
        return (jnp.maximum(s - 1, 0), 0, 0)

    def conv_tile(s):
        t = jnp.minimum(s, n_tiles - 1)
        return t // tiles_per_seq, t % tiles_per_seq

    def u_block(s):
        b, i = conv_tile(s)
        return (b, 0, i, 0)

    def halo_block(s):
        b, i = conv_tile(s)
        return (b, 0, jnp.maximum(i * halo_blocks - 1, 0), 0)

    const = lambda s: (0, 0)
    resident = functools.partial(pl.BlockSpec, index_map=const, pipeline_mode=pl.Buffered(1))
    out = pl.pallas_call(
        functools.partial(_ffn_kernel, tiles_per_seq=tiles_per_seq),
        grid=(n_tiles + 1,),
        in_specs=[
            pl.BlockSpec((1, tm, D_MODEL), ffn_tile),
            pl.BlockSpec((1, tm, D_ATTN), ffn_tile),
            pl.BlockSpec((1, D_CONV // LANES, tm, LANES), u_block),
            pl.BlockSpec((1, D_CONV // LANES, CONV_HALO, LANES), halo_block),
            pl.BlockSpec((CONV_WIDTH, D_CONV), const),
            pl.BlockSpec((1, D_CONV), const),
            pl.BlockSpec((1, D_CONV), const),
            pl.BlockSpec((1, D_CONV), const),
            resident((D_MODEL, D_MODEL)),
            pl.BlockSpec((1, D_MODEL), const),
            resident((D_MODEL, D_FF)),
            resident((D_FF, D_MODEL)),
            pl.BlockSpec((1, D_MODEL), const),
        ],
        out_specs=pl.BlockSpec((1, tm, D_MODEL), ffn_tile),
        out_shape=jax.ShapeDtypeStruct((n_tiles, tm, D_MODEL), jnp.float32),
        scratch_shapes=[
            pltpu.VMEM((D_CONV // LANES, CONV_HALO + tm, LANES), jnp.float32),
            pltpu.VMEM((tm, D_CONV), jnp.bfloat16),
        ],
        compiler_params=pltpu.CompilerParams(
            dimension_semantics=("arbitrary",),
            vmem_limit_bytes=VMEM_LIMIT_BYTES),
        name="conv_outproj_ffn",
    )(x, attn, u, u, conv_w, conv_b, ln_g, ln_b, w_o, g2, w1, w2, gf)
    return out.reshape(B, S, D_MODEL)


def kernel(x, norm1_g, w_in, conv_w, conv_b, conv_ln_g, conv_ln_b, w_o, norm2_g, w_ff1, w_ff2,
           rel_bias, final_g):
    assert norm1_g.shape[0] == 1, "single-layer problem"
    bf16 = jnp.bfloat16
    nat, d4, d16, u = _project(x, norm1_g, w_in[0].astype(bf16))
    attn = _attention(nat, d4, d16, _bias_tables(rel_bias))
    return _ffn(x, attn, u, conv_w[0], conv_b, conv_ln_g, conv_ln_b, w_o[0].astype(bf16), norm2_g,
                w_ff1[0].astype(bf16), w_ff2[0].astype(bf16), final_g.reshape(1, D_MODEL))
```

```python
import functools
import math

import jax
import jax.numpy as jnp
from jax import lax
from jax.experimental import pallas as pl
from jax.experimental.pallas import tpu as pltpu

D_MODEL = 1024
N_HEADS = 8
HEAD_DIM = 64
D_ATTN = N_HEADS * HEAD_DIM
D_CONV = D_MODEL - D_ATTN
D_QKV = 3 * D_ATTN
D_IN_PROJ = D_QKV + 2 * D_CONV
CONV_WIDTH = 31
D_FF = 4 * D_MODEL
BRANCHES = ((128, 1), (512, 4), (2048, 16))
WIN = 128
NUM_BUCKETS = 32
MAX_DISTANCE = 2048
RMS_EPS = 1e-6
LN_EPS = 1e-5
NEG_INF = -1e30
LOG2E = math.log2(math.e)

LANES = 128
HEADS_PER_GROUP = LANES // HEAD_DIM
N_GROUPS = N_HEADS // HEADS_PER_GROUP
VMEM_LIMIT_BYTES = 56 * 1024 * 1024

PROJ_ROWS = 512
CONV_HALO = 32
CONV_CHUNK = 32
RESIDUE_PITCH = 136
MERGE_ROWS = 256
FFN_ROWS = 512
FF_CHUNK = 512


def _proj_kernel(x_ref, g1_ref, win_ref, nat_ref, d4_ref, d16_ref, u_ref, slab_ref):
    tm = PROJ_ROWS
    x = x_ref[0]
    inv = lax.rsqrt(jnp.mean(x * x, axis=-1, keepdims=True) + RMS_EPS)
    h = (x * inv * g1_ref[...]).astype(jnp.bfloat16)

    for g in range(D_CONV // 256):
        a = jnp.dot(h, win_ref[:, D_QKV + g * 256:D_QKV + (g + 1) * 256],
                    preferred_element_type=jnp.float32)
        gate = jnp.dot(h, win_ref[:, D_QKV + D_CONV + g * 256:D_QKV + D_CONV + (g + 1) * 256],
                       preferred_element_type=jnp.float32)
        u = a * jax.nn.sigmoid(gate)
        for half in range(2):
            u_ref[0, 2 * g + half] = u[:, half * LANES:(half + 1) * LANES]

    q_scale = LOG2E / math.sqrt(HEAD_DIM)
    for g in range(D_QKV // 256):
        z = jnp.dot(h, win_ref[:, g * 256:(g + 1) * 256], preferred_element_type=jnp.float32)
        if g * 256 < D_ATTN:
            z = z * q_scale
        for half in range(2):
            s = 2 * g + half
            zs = z[:, half * LANES:(half + 1) * LANES]
            slab_ref[s] = zs
            nat_ref[0, :, s * LANES:(s + 1) * LANES] = zs.astype(jnp.bfloat16)
            cols = slice(s * LANES, (s + 1) * LANES)
            for r in range(4):
                d4_ref[0, r, :, cols] = slab_ref[s, pl.ds(r, tm // 4, stride=4), :].astype(jnp.bfloat16)
            for r in range(16):
                d16_ref[0, r, :, cols] = slab_ref[s, pl.ds(r, tm // 16, stride=16), :].astype(jnp.bfloat16)


def _project(x, g1, w_in):
    B, S, _ = x.shape
    tm = PROJ_ROWS
    const = lambda b, i: (0, 0)
    return pl.pallas_call(
        _proj_kernel,
        grid=(B, S // tm),
        in_specs=[
            pl.BlockSpec((1, tm, D_MODEL), lambda b, i: (b, i, 0)),
            pl.BlockSpec((1, D_MODEL), const),
            pl.BlockSpec((D_MODEL, D_IN_PROJ), const),
        ],
        out_specs=[
            pl.BlockSpec((1, tm, D_QKV), lambda b, i: (b, i, 0)),
            pl.BlockSpec((1, 4, tm // 4, D_QKV), lambda b, i: (b, 0, i, 0)),
            pl.BlockSpec((1, 16, tm // 16, D_QKV), lambda b, i: (b, 0, i, 0)),
            pl.BlockSpec((1, D_CONV // LANES, tm, LANES), lambda b, i: (b, 0, i, 0)),
        ],
        out_shape=[
            jax.ShapeDtypeStruct((B, S, D_QKV), jnp.bfloat16),
            jax.ShapeDtypeStruct((B, 4, S // 4, D_QKV), jnp.bfloat16),
            jax.ShapeDtypeStruct((B, 16, S // 16, D_QKV), jnp.bfloat16),
            jax.ShapeDtypeStruct((B, D_CONV // LANES, S, LANES), jnp.float32),
        ],
        scratch_shapes=[pltpu.VMEM((D_QKV // LANES, tm, LANES), jnp.float32)],
        compiler_params=pltpu.CompilerParams(
            dimension_semantics=("parallel", "parallel"),
            vmem_limit_bytes=VMEM_LIMIT_BYTES),
        name="in_proj",
    )(x, g1, w_in)


def _attn_kernel(qn_ref, kn_ref, vn_ref, q4_ref, k4_ref, v4_ref, q16_ref, k16_ref, v16_ref,
                 bias_ref, out_ref, o_ref, l_ref, m_ref):
    assert HEADS_PER_GROUP == 2
    S = out_ref.shape[1]
    lane = lax.broadcasted_iota(jnp.int32, (1, LANES), 1)
    head_lanes = [(lane >= h * HEAD_DIM) & (lane < (h + 1) * HEAD_DIM) for h in range(HEADS_PER_GROUP)]
    head_bits = [jnp.where(hl, jnp.uint32(0xFFFFFFFF), jnp.uint32(0)) for hl in head_lanes]
    head_ones = [jnp.where(hl, 1.0, 0.0).astype(jnp.bfloat16) for hl in head_lanes]

    def keep_head(x, h):
        return pltpu.bitcast(pltpu.bitcast(x, jnp.uint32) & head_bits[h], jnp.bfloat16)

    def window_block(br, q_ref, k_ref, v_ref, blk, first, dst_start, dst_stride):
        nk = WIN if first else 2 * WIN
        q0 = blk * WIN
        k0 = q0 if first else q0 - WIN
        q2 = q_ref[pl.ds(q0, WIN), :]
        k2 = k_ref[pl.ds(k0, nk), :]
        v2 = v_ref[pl.ds(k0, nk), :]
        q_both = jnp.concatenate([keep_head(q2, h) for h in range(HEADS_PER_GROUP)], axis=0)
        s = lax.dot_general(q_both, k2, (((1,), (1,)), ((), ())), preferred_element_type=jnp.float32)
        s = s + bias_ref[br, :, :, 2 * WIN - nk:2 * WIN].reshape(HEADS_PER_GROUP * WIN, nk)
        m = jnp.max(s, axis=-1, keepdims=True)
        p = jnp.exp2(s - m).astype(jnp.bfloat16)
        p_cat = jnp.concatenate([p[h * WIN:(h + 1) * WIN] for h in range(HEADS_PER_GROUP)], axis=1)
        v_cat = jnp.concatenate(
            [jnp.concatenate([keep_head(v2, h), jnp.broadcast_to(head_ones[h], (nk, LANES))], axis=1)
             for h in range(HEADS_PER_GROUP)], axis=0)
        r = jnp.dot(p_cat, v_cat, preferred_element_type=jnp.float32)
        o_ref[br, rows(dst_start, dst_stride), :] = r[:, :LANES]
        l_ref[br, rows(dst_start, dst_stride), :] = r[:, LANES:]
        m_ref[br, rows(dst_start, dst_stride), :] = jnp.where(head_lanes[0], m[:WIN], m[WIN:])

    def rows(start, stride):
        return pl.ds(start, WIN) if stride == 1 else pl.ds(start, WIN, stride=stride)

    srcs = ((qn_ref, kn_ref, vn_ref), (q4_ref, k4_ref, v4_ref), (q16_ref, k16_ref, v16_ref))
    gathered = []
    for br, (_, dil) in enumerate(BRANCHES):
        q_ref, k_ref, v_ref = (r.at[0] for r in srcs[br])
        nb = S // dil // WIN
        gathered.append(dil % 8 == 0)
        assert not gathered[br] or nb == 1
        for first in (True, False):
            for res in range(dil):
                for n in range(0 if first else 1, 1 if first else nb):
                    dst = (res * RESIDUE_PITCH, 1) if gathered[br] else (res + dil * WIN * n, dil)
                    window_block(br, q_ref, k_ref, v_ref, res * nb + n, first, *dst)

    def merge(c, carry):
        def branch_rows(ref, br):
            if not gathered[br]:
                return ref[br, pl.ds(pl.multiple_of(c * MERGE_ROWS, MERGE_ROWS), MERGE_ROWS), :]
            dil = BRANCHES[br][1]
            j0 = c * (MERGE_ROWS // dil)
            return jnp.concatenate(
                [ref[br, pl.ds(j0 + i, dil, stride=RESIDUE_PITCH), :] for i in range(MERGE_ROWS // dil)],
                axis=0)

        ms = [branch_rows(m_ref, br) for br in range(3)]
        m_all = jnp.maximum(jnp.maximum(ms[0], ms[1]), ms[2])
        ws = [jnp.exp2(mb - m_all) for mb in ms]
        num = sum(ws[br] * branch_rows(o_ref, br) for br in range(3))
        den = sum(ws[br] * branch_rows(l_ref, br) for br in range(3))
        out_rows = pl.ds(pl.multiple_of(c * MERGE_ROWS, MERGE_ROWS), MERGE_ROWS)
        out_ref[0, out_rows, :] = (num / den).astype(out_ref.dtype)
        return carry

    lax.fori_loop(0, S // MERGE_ROWS, merge, 0)


def _attention(nat, d4, d16, bias):
    B, S, _ = nat.shape
    d4 = d4.reshape(B, S, D_QKV)
    d16 = d16.reshape(B, S, D_QKV)
    qkv_specs = []
    for _ in range(3):
        for part in range(3):
            qkv_specs.append(pl.BlockSpec((1, S, LANES),
                                          lambda b, p, part=part: (b, 0, part * N_GROUPS + p)))
    return pl.pallas_call(
        _attn_kernel,
        grid=(B, N_GROUPS),
        in_specs=qkv_specs + [
            pl.BlockSpec((3, HEADS_PER_GROUP, WIN, 2 * WIN), lambda b, p: (0, p, 0, 0)),
        ],
        out_specs=pl.BlockSpec((1, S, LANES), lambda b, p: (b, 0, p)),
        out_shape=jax.ShapeDtypeStruct((B, S, D_ATTN), jnp.bfloat16),
        scratch_shapes=[pltpu.VMEM((3, max(S, 16 * RESIDUE_PITCH), LANES), jnp.float32)] * 3,
        compiler_params=pltpu.CompilerParams(
            dimension_semantics=("parallel", "parallel"),
            vmem_limit_bytes=VMEM_LIMIT_BYTES),
        name="dilated_attn",
    )(nat, nat, nat, d4, d4, d4, d16, d16, d16, bias)


def _t5_causal_bucket(distance):
    max_exact = NUM_BUCKETS // 2
    d = jnp.maximum(distance, 1).astype(jnp.float32)
    large = max_exact + (jnp.log(d / max_exact) / math.log(MAX_DISTANCE / max_exact)
                         * (NUM_BUCKETS - max_exact)).astype(jnp.int32)
    large = jnp.minimum(large, NUM_BUCKETS - 1)
    return jnp.where(distance < max_exact, distance, large)


def _bias_tables(rel_bias):
    qi = jnp.arange(WIN, dtype=jnp.int32)[:, None]
    kc = jnp.arange(2 * WIN, dtype=jnp.int32)[None, :]
    dist = qi + WIN - kc
    valid = (dist >= 0) & (dist <= WIN)
    tables = []
    for _, dil in BRANCHES:
        bucket = _t5_causal_bucket(jnp.clip(dist, 0, WIN) * dil)
        bias = jnp.zeros((N_HEADS, WIN, 2 * WIN), jnp.float32)
        for b in range(NUM_BUCKETS):
            bias = jnp.where(bucket[None] == b, rel_bias[b].astype(jnp.float32)[:, None, None], bias)
        tables.append(jnp.where(valid[None], bias * LOG2E, NEG_INF))
    return jnp.stack(tables)


def _ffn_kernel(x_ref, attn_ref, u_ref, uh_ref, cw_ref, cb_ref, lng_ref, lnb_ref,
                wo_ref, g2_ref, w1_ref, w2_ref, gf_ref, out_ref, ubuf_ref, conv_ref, *, tiles_per_seq):
    tm = FFN_ROWS
    step = pl.program_id(0)
    n_slabs = D_CONV // LANES

    @pl.when(step == 0)
    def _():
        conv_ref[...] = jnp.zeros_like(conv_ref)

    mixed = jnp.concatenate([attn_ref[0], conv_ref[...]], axis=-1)
    x1 = x_ref[0] + jnp.dot(mixed, wo_ref[...], preferred_element_type=jnp.float32)
    inv = lax.rsqrt(jnp.mean(x1 * x1, axis=-1, keepdims=True) + RMS_EPS)
    h = (x1 * inv * g2_ref[...]).astype(jnp.bfloat16)

    seq_start = (step % tiles_per_seq) == 0
    ubuf_ref[:, 0:CONV_HALO, :] = jnp.where(seq_start, 0.0, uh_ref[0])
    ubuf_ref[:, CONV_HALO:CONV_HALO + tm, :] = u_ref[0]
    off = CONV_HALO - (CONV_WIDTH - 1)

    def exact_zero(tile):
        half_word = jnp.uint32(16)
        bits = lax.shift_right_logical(pltpu.bitcast(tile, jnp.uint32), half_word)
        return pltpu.bitcast(lax.shift_right_logical(bits, half_word), jnp.float32)

    def conv_chunk(c):
        base = c * CONV_CHUNK
        slabs = []
        for s in range(n_slabs):
            cols = slice(s * LANES, (s + 1) * LANES)
            a = jnp.zeros((CONV_CHUNK, LANES), jnp.float32) + cb_ref[:, cols]
            for j in range(CONV_WIDTH):
                tap = ubuf_ref[s, pl.ds(base + off + j, CONV_CHUNK, stride=1), :]
                a = a + tap * cw_ref[j:j + 1, cols]
            slabs.append(a)
        y = jnp.concatenate(slabs, axis=-1)
        mu = jnp.mean(y, axis=-1, keepdims=True)
        cen = y - mu
        var = jnp.mean(cen * cen, axis=-1, keepdims=True)
        y = cen * lax.rsqrt(var + LN_EPS) * lng_ref[...] + lnb_ref[...]
        y = y * jax.nn.sigmoid(y)
        conv_ref[base:base + CONV_CHUNK, :] = y.astype(jnp.bfloat16)
        fold = y[0:8, 0:LANES]
        for r in range(CONV_CHUNK // 8):
            for s in range(n_slabs):
                if r or s:
                    fold = jnp.maximum(fold, y[8 * r:8 * r + 8, s * LANES:(s + 1) * LANES])
        return exact_zero(fold)

    n_ff = D_FF // FF_CHUNK
    chunk_ids = iter(range(tm // CONV_CHUNK))
    anchor_cols = FF_CHUNK * n_ff // (tm // CONV_CHUNK)
    assert anchor_cols % LANES == 0

    def anchored(d):
        top = []
        for g in range(d.shape[1] // anchor_cols):
            top.append(d[0:8, anchor_cols * g:anchor_cols * g + LANES] + conv_chunk(next(chunk_ids)))
            if anchor_cols > LANES:
                top.append(d[0:8, anchor_cols * g + LANES:anchor_cols * (g + 1)])
        return jnp.concatenate([jnp.concatenate(top, axis=1), d[8:]], axis=0)

    acc = x1
    for c in range(n_ff):
        cols = slice(c * FF_CHUNK, (c + 1) * FF_CHUNK)
        f = jnp.maximum(anchored(jnp.dot(h, w1_ref[:, cols], preferred_element_type=jnp.float32)), 0.0)
        f = (f * f).astype(jnp.bfloat16)
        acc = acc + jnp.dot(f, w2_ref[cols, :], preferred_element_type=jnp.float32)
    inv = lax.rsqrt(jnp.mean(acc * acc, axis=-1, keepdims=True) + RMS_EPS)
    out_ref[0] = acc * inv * gf_ref[...]


def _ffn(x, attn, u, conv_w, conv_b, ln_g, ln_b, w_o, g2, w1, w2, gf):
    B, S, _ = x.shape
    tm = FFN_ROWS
    tiles_per_seq = S // tm
    n_tiles = B * tiles_per_seq
    halo_blocks = tm // CONV_HALO
    x = x.reshape(n_tiles, tm, D_MODEL)
    attn = attn.reshape(n_tiles, tm, D_ATTN)

    def ffn_tile(s):
        return (jnp.maximum(s - 1, 0), 0, 0)

    def conv_tile(s):
        t = jnp.minimum(s, n_tiles - 1)
        return t // tiles_per_seq, t % tiles_per_seq

    def u_block(s):
        b, i = conv_tile(s)
        return (b, 0, i, 0)

    def halo_block(s):
        b, i = conv_tile(s)
        return (b, 0, jnp.maximum(i * halo_blocks - 1, 0), 0)

    const = lambda s: (0, 0)
    resident = functools.partial(pl.BlockSpec, index_map=const, pipeline_mode=pl.Buffered(1))
    out = pl.pallas_call(
        functools.partial(_ffn_kernel, tiles_per_seq=tiles_per_seq),
        grid=(n_tiles + 1,),
        in_specs=[
            pl.BlockSpec((1, tm, D_MODEL), ffn_tile),
            pl.BlockSpec((1, tm, D_ATTN), ffn_tile),
            pl.BlockSpec((1, D_CONV // LANES, tm, LANES), u_block),
            pl.BlockSpec((1, D_CONV // LANES, CONV_HALO, LANES), halo_block),
            pl.BlockSpec((CONV_WIDTH, D_CONV), const),
            pl.BlockSpec((1, D_CONV), const),
            pl.BlockSpec((1, D_CONV), const),
            pl.BlockSpec((1, D_CONV), const),
            resident((D_MODEL, D_MODEL)),
            pl.BlockSpec((1, D_MODEL), const),
            resident((D_MODEL, D_FF)),
            resident((D_FF, D_MODEL)),
            pl.BlockSpec((1, D_MODEL), const),
        ],
        out_specs=pl.BlockSpec((1, tm, D_MODEL), ffn_tile),
        out_shape=jax.ShapeDtypeStruct((n_tiles, tm, D_MODEL), jnp.float32),
        scratch_shapes=[
            pltpu.VMEM((D_CONV // LANES, CONV_HALO + tm, LANES), jnp.float32),
            pltpu.VMEM((tm, D_CONV), jnp.bfloat16),
        ],
        compiler_params=pltpu.CompilerParams(
            dimension_semantics=("arbitrary",),
            vmem_limit_bytes=VMEM_LIMIT_BYTES),
        name="conv_outproj_ffn",
    )(x, attn, u, u, conv_w, conv_b, ln_g, ln_b, w_o, g2, w1, w2, gf)
    return out.reshape(B, S, D_MODEL)


def kernel(x, norm1_g, w_in, conv_w, conv_b, conv_ln_g, conv_ln_b, w_o, norm2_g, w_ff1, w_ff2,
           rel_bias, final_g):
    assert norm1_g.shape[0] == 1, "single-layer problem"
    bf16 = jnp.bfloat16
    nat, d4, d16, u = _project(x, norm1_g, w_in[0].astype(bf16))
    attn = _attention(nat, d4, d16, _bias_tables(rel_bias))
    return _ffn(x, attn, u, conv_w[0], conv_b, conv_ln_g, conv_ln_b, w_o[0].astype(bf16), norm2_g,
                w_ff1[0].astype(bf16), w_ff2[0].astype(bf16), final_g.reshape(1, D_MODEL))
```

```python
import functools
import math

import jax
import jax.numpy as jnp
from jax import lax
from jax.experimental import pallas as pl
from jax.experimental.pallas import tpu as pltpu

D_MODEL = 1024
N_HEADS = 8
HEAD_DIM = 64
D_ATTN = N_HEADS * HEAD_DIM
D_CONV = D_MODEL - D_ATTN
D_QKV = 3 * D_ATTN
D_IN_PROJ = D_QKV + 2 * D_CONV
CONV_WIDTH = 31
D_FF = 4 * D_MODEL
BRANCHES = ((128, 1), (512, 4), (2048, 16))
WIN = 128
NUM_BUCKETS = 32
MAX_DISTANCE = 2048
RMS_EPS = 1e-6
LN_EPS = 1e-5
NEG_INF = -1e30
LOG2E = math.log2(math.e)

LANES = 128
HEADS_PER_GROUP = LANES // HEAD_DIM
N_GROUPS = N_HEADS // HEADS_PER_GROUP
VMEM_LIMIT_BYTES = 56 * 1024 * 1024

PROJ_ROWS = 512
CONV_HALO = 32
CONV_CHUNK = 32
SUB_DIL = 4
RESIDUE_PITCH = 136
MERGE_ROWS = 256
FFN_ROWS = 512
FF_CHUNK = 512


def _proj_kernel(x_ref, g1_ref, win_ref, qkv_ref, u_ref):
    x = x_ref[0]
    inv = lax.rsqrt(jnp.mean(x * x, axis=-1, keepdims=True) + RMS_EPS)
    h = (x * inv * g1_ref[...]).astype(jnp.bfloat16)

    for g in range(D_CONV // 256):
        a = jnp.dot(h, win_ref[:, D_QKV + g * 256:D_QKV + (g + 1) * 256],
                    preferred_element_type=jnp.float32)
        gate = jnp.dot(h, win_ref[:, D_QKV + D_CONV + g * 256:D_QKV + D_CONV + (g + 1) * 256],
                       preferred_element_type=jnp.float32)
        u = a * jax.nn.sigmoid(gate)
        for half in range(2):
            u_ref[0, 2 * g + half] = u[:, half * LANES:(half + 1) * LANES]

    q_scale = LOG2E / math.sqrt(HEAD_DIM)
    for g in range(D_QKV // 256):
        z = jnp.dot(h, win_ref[:, g * 256:(g + 1) * 256], preferred_element_type=jnp.float32)
        if g * 256 < D_ATTN:
            z = z * q_scale
        qkv_ref[0, :, g * 256:(g + 1) * 256] = z


def _project(x, g1, w_in):
    B, S, _ = x.shape
    tm = PROJ_ROWS
    const = lambda b, i: (0, 0)
    return pl.pallas_call(
        _proj_kernel,
        grid=(B, S // tm),
        in_specs=[
            pl.BlockSpec((1, tm, D_MODEL), lambda b, i: (b, i, 0)),
            pl.BlockSpec((1, D_MODEL), const),
            pl.BlockSpec((D_MODEL, D_IN_PROJ), const),
        ],
        out_specs=[
            pl.BlockSpec((1, tm, D_QKV), lambda b, i: (b, i, 0)),
            pl.BlockSpec((1, D_CONV // LANES, tm, LANES), lambda b, i: (b, 0, i, 0)),
        ],
        out_shape=[
            jax.ShapeDtypeStruct((B, S, D_QKV), jnp.float32),
            jax.ShapeDtypeStruct((B, D_CONV // LANES, S, LANES), jnp.float32),
        ],
        compiler_params=pltpu.CompilerParams(
            dimension_semantics=("parallel", "parallel"),
            vmem_limit_bytes=VMEM_LIMIT_BYTES),
        name="in_proj",
    )(x, g1, w_in)


def _attn_kernel(q_ref, k_ref, v_ref, bias_ref, out_ref, sub_ref, o_ref, l_ref, m_ref):
    assert HEADS_PER_GROUP == 2
    S = out_ref.shape[1]
    nat_refs = (q_ref.at[0], k_ref.at[0], v_ref.at[0])
    sub_len = S // SUB_DIL

    for a in range(3):
        for r in range(SUB_DIL):
            sub_ref[a, r * sub_len:(r + 1) * sub_len, :] = nat_refs[a][pl.ds(r, sub_len, stride=SUB_DIL), :]

    def load_rows(a, dil, res, j0, n):
        if dil == 1:
            x = nat_refs[a][pl.ds(j0, n), :]
        else:
            step = dil // SUB_DIL
            start = (res % SUB_DIL) * sub_len + step * j0 + res // SUB_DIL
            x = sub_ref[a, pl.ds(start, n), :] if step == 1 else sub_ref[a, pl.ds(start, n, stride=step), :]
        return x.astype(jnp.bfloat16)

    lane = lax.broadcasted_iota(jnp.int32, (1, LANES), 1)
    head_lanes = [(lane >= h * HEAD_DIM) & (lane < (h + 1) * HEAD_DIM) for h in range(HEADS_PER_GROUP)]
    head_bits = [jnp.where(hl, jnp.uint32(0xFFFFFFFF), jnp.uint32(0)) for hl in head_lanes]
    head_ones = [jnp.where(hl, 1.0, 0.0).astype(jnp.bfloat16) for hl in head_lanes]

    def keep_head(x, h):
        return pltpu.bitcast(pltpu.bitcast(x, jnp.uint32) & head_bits[h], jnp.bfloat16)

    def window_block(br, dil, res, n, dst_start, dst_stride):
        first = n == 0
        nk = WIN if first else 2 * WIN
        k0 = n * WIN if first else (n - 1) * WIN
        q2 = load_rows(0, dil, res, n * WIN, WIN)
        k2 = load_rows(1, dil, res, k0, nk)
        v2 = load_rows(2, dil, res, k0, nk)
        q_both = jnp.concatenate([keep_head(q2, h) for h in range(HEADS_PER_GROUP)], axis=0)
        s = lax.dot_general(q_both, k2, (((1,), (1,)), ((), ())), preferred_element_type=jnp.float32)
        s = s + bias_ref[br, :, :, 2 * WIN - nk:2 * WIN].reshape(HEADS_PER_GROUP * WIN, nk)
        m = jnp.max(s, axis=-1, keepdims=True)
        p = jnp.exp2(s - m).astype(jnp.bfloat16)
        p_cat = jnp.concatenate([p[h * WIN:(h + 1) * WIN] for h in range(HEADS_PER_GROUP)], axis=1)
        v_cat = jnp.concatenate(
            [jnp.concatenate([keep_head(v2, h), jnp.broadcast_to(head_ones[h], (nk, LANES))], axis=1)
             for h in range(HEADS_PER_GROUP)], axis=0)
        r = jnp.dot(p_cat, v_cat, preferred_element_type=jnp.float32)
        o_ref[br, rows(dst_start, dst_stride), :] = r[:, :LANES]
        l_ref[br, rows(dst_start, dst_stride), :] = r[:, LANES:]
        m_ref[br, rows(dst_start, dst_stride), :] = jnp.where(head_lanes[0], m[:WIN], m[WIN:])

    def rows(start, stride):
        return pl.ds(start, WIN) if stride == 1 else pl.ds(start, WIN, stride=stride)

    gathered = []
    for br, (_, dil) in enumerate(BRANCHES):
        assert dil == 1 or dil % SUB_DIL == 0
        nb = S // dil // WIN
        gathered.append(dil % 8 == 0)
        assert not gathered[br] or nb == 1
        for first in (True, False):
            for res in range(dil):
                for n in range(0 if first else 1, 1 if first else nb):
                    dst = (res * RESIDUE_PITCH, 1) if gathered[br] else (res + dil * WIN * n, dil)
                    window_block(br, dil, res, n, *dst)

    def merge(c, carry):
        def branch_rows(ref, br):
            if not gathered[br]:
                return ref[br, pl.ds(pl.multiple_of(c * MERGE_ROWS, MERGE_ROWS), MERGE_ROWS), :]
            dil = BRANCHES[br][1]
            j0 = c * (MERGE_ROWS // dil)
            return jnp.concatenate(
                [ref[br, pl.ds(j0 + i, dil, stride=RESIDUE_PITCH), :] for i in range(MERGE_ROWS // dil)],
                axis=0)

        ms = [branch_rows(m_ref, br) for br in range(3)]
        m_all = jnp.maximum(jnp.maximum(ms[0], ms[1]), ms[2])
        ws = [jnp.exp2(mb - m_all) for mb in ms]
        num = sum(ws[br] * branch_rows(o_ref, br) for br in range(3))
        den = sum(ws[br] * branch_rows(l_ref, br) for br in range(3))
        out_rows = pl.ds(pl.multiple_of(c * MERGE_ROWS, MERGE_ROWS), MERGE_ROWS)
        out_ref[0, out_rows, :] = (num / den).astype(out_ref.dtype)
        return carry

    lax.fori_loop(0, S // MERGE_ROWS, merge, 0)


def _attention(qkv, bias):
    B, S, _ = qkv.shape
    qkv_specs = [pl.BlockSpec((1, S, LANES), lambda b, p, part=part: (b, 0, part * N_GROUPS + p))
                 for part in range(3)]
    return pl.pallas_call(
        _attn_kernel,
        grid=(B, N_GROUPS),
        in_specs=qkv_specs + [
            pl.BlockSpec((3, HEADS_PER_GROUP, WIN, 2 * WIN), lambda b, p: (0, p, 0, 0)),
        ],
        out_specs=pl.BlockSpec((1, S, LANES), lambda b, p: (b, 0, p)),
        out_shape=jax.ShapeDtypeStruct((B, S, D_ATTN), jnp.bfloat16),
        scratch_shapes=[pltpu.VMEM((3, S, LANES), jnp.float32)] +
                       [pltpu.VMEM((3, max(S, 16 * RESIDUE_PITCH), LANES), jnp.float32)] * 3,
        compiler_params=pltpu.CompilerParams(
            dimension_semantics=("parallel", "parallel"),
            vmem_limit_bytes=VMEM_LIMIT_BYTES),
        name="dilated_attn",
    )(qkv, qkv, qkv, bias)


def _t5_causal_bucket(distance):
    max_exact = NUM_BUCKETS // 2
    d = jnp.maximum(distance, 1).astype(jnp.float32)
    large = max_exact + (jnp.log(d / max_exact) / math.log(MAX_DISTANCE / max_exact)
                         * (NUM_BUCKETS - max_exact)).astype(jnp.int32)
    large = jnp.minimum(large, NUM_BUCKETS - 1)
    return jnp.where(distance < max_exact, distance, large)


def _bias_tables(rel_bias):
    qi = jnp.arange(WIN, dtype=jnp.int32)[:, None]
    kc = jnp.arange(2 * WIN, dtype=jnp.int32)[None, :]
    dist = qi + WIN - kc
    valid = (dist >= 0) & (dist <= WIN)
    tables = []
    for _, dil in BRANCHES:
        bucket = _t5_causal_bucket(jnp.clip(dist, 0, WIN) * dil)
        bias = jnp.zeros((N_HEADS, WIN, 2 * WIN), jnp.float32)
        for b in range(NUM_BUCKETS):
            bias = jnp.where(bucket[None] == b, rel_bias[b].astype(jnp.float32)[:, None, None], bias)
        tables.append(jnp.where(valid[None], bias * LOG2E, NEG_INF))
    return jnp.stack(tables)


def _ffn_kernel(x_ref, attn_ref, u_ref, uh_ref, cw_ref, cb_ref, lng_ref, lnb_ref,
                wo_ref, g2_ref, w1_ref, w2_ref, gf_ref, out_ref, ubuf_ref, conv_ref, *, tiles_per_seq):
    tm = FFN_ROWS
    step = pl.program_id(0)
    n_slabs = D_CONV // LANES

    @pl.when(step == 0)
    def _():
        conv_ref[...] = jnp.zeros_like(conv_ref)

    mixed = jnp.concatenate([attn_ref[0], conv_ref[...]], axis=-1)
    x1 = x_ref[0] + jnp.dot(mixed, wo_ref[...], preferred_element_type=jnp.float32)
    inv = lax.rsqrt(jnp.mean(x1 * x1, axis=-1, keepdims=True) + RMS_EPS)
    h = (x1 * inv * g2_ref[...]).astype(jnp.bfloat16)

    seq_start = (step % tiles_per_seq) == 0
    ubuf_ref[:, 0:CONV_HALO, :] = jnp.where(seq_start, 0.0, uh_ref[0])
    ubuf_ref[:, CONV_HALO:CONV_HALO + tm, :] = u_ref[0]
    off = CONV_HALO - (CONV_WIDTH - 1)

    def exact_zero(tile):
        half_word = jnp.uint32(16)
        bits = lax.shift_right_logical(pltpu.bitcast(tile, jnp.uint32), half_word)
        return pltpu.bitcast(lax.shift_right_logical(bits, half_word), jnp.float32)

    def conv_chunk(c):
        base = c * CONV_CHUNK
        slabs = []
        for s in range(n_slabs):
            cols = slice(s * LANES, (s + 1) * LANES)
            a = jnp.zeros((CONV_CHUNK, LANES), jnp.float32) + cb_ref[:, cols]
            for j in range(CONV_WIDTH):
                tap = ubuf_ref[s, pl.ds(base + off + j, CONV_CHUNK, stride=1), :]
                a = a + tap * cw_ref[j:j + 1, cols]
            slabs.append(a)
        y = jnp.concatenate(slabs, axis=-1)
        mu = jnp.mean(y, axis=-1, keepdims=True)
        cen = y - mu
        var = jnp.mean(cen * cen, axis=-1, keepdims=True)
        y = cen * lax.rsqrt(var + LN_EPS) * lng_ref[...] + lnb_ref[...]
        y = y * jax.nn.sigmoid(y)
        conv_ref[base:base + CONV_CHUNK, :] = y.astype(jnp.bfloat16)
        fold = y[0:8, 0:LANES]
        for r in range(CONV_CHUNK // 8):
            for s in range(n_slabs):
                if r or s:
                    fold = jnp.maximum(fold, y[8 * r:8 * r + 8, s * LANES:(s + 1) * LANES])
        return exact_zero(fold)

    n_ff = D_FF // FF_CHUNK
    chunk_ids = iter(range(tm // CONV_CHUNK))
    anchor_cols = FF_CHUNK * n_ff // (tm // CONV_CHUNK)
    assert anchor_cols % LANES == 0

    def anchored(d):
        top = []
        for g in range(d.shape[1] // anchor_cols):
            top.append(d[0:8, anchor_cols * g:anchor_cols * g + LANES] + conv_chunk(next(chunk_ids)))
            if anchor_cols > LANES:
                top.append(d[0:8, anchor_cols * g + LANES:anchor_cols * (g + 1)])
        return jnp.concatenate([jnp.concatenate(top, axis=1), d[8:]], axis=0)

    acc = x1
    for c in range(n_ff):
        cols = slice(c * FF_CHUNK, (c + 1) * FF_CHUNK)
        f = jnp.maximum(anchored(jnp.dot(h, w1_ref[:, cols], preferred_element_type=jnp.float32)), 0.0)
        f = (f * f).astype(jnp.bfloat16)
        acc = acc + jnp.dot(f, w2_ref[cols, :], preferred_element_type=jnp.float32)
    inv = lax.rsqrt(jnp.mean(acc * acc, axis=-1, keepdims=True) + RMS_EPS)
    out_ref[0] = acc * inv * gf_ref[...]


def _ffn(x, attn, u, conv_w, conv_b, ln_g, ln_b, w_o, g2, w1, w2, gf):
    B, S, _ = x.shape
    tm = FFN_ROWS
    tiles_per_seq = S // tm
    n_tiles = B * tiles_per_seq
    halo_blocks = tm // CONV_HALO
    x = x.reshape(n_tiles, tm, D_MODEL)
    attn = attn.reshape(n_tiles, tm, D_ATTN)

    def ffn_tile(s):
        return (jnp.maximum(s - 1, 0), 0, 0)

    def conv_tile(s):
        t = jnp.minimum(s, n_tiles - 1)
        return t // tiles_per_seq, t % tiles_per_seq

    def u_block(s):
        b, i = conv_tile(s)
        return (b, 0, i, 0)

    def halo_block(s):
        b, i = conv_tile(s)
        return (b, 0, jnp.maximum(i * halo_blocks - 1, 0), 0)

    const = lambda s: (0, 0)
    resident = functools.partial(pl.BlockSpec, index_map=const, pipeline_mode=pl.Buffered(1))
    out = pl.pallas_call(
        functools.partial(_ffn_kernel, tiles_per_seq=tiles_per_seq),
        grid=(n_tiles + 1,),
        in_specs=[
            pl.BlockSpec((1, tm, D_MODEL), ffn_tile),
            pl.BlockSpec((1, tm, D_ATTN), ffn_tile),
            pl.BlockSpec((1, D_CONV // LANES, tm, LANES), u_block),
            pl.BlockSpec((1, D_CONV // LANES, CONV_HALO, LANES), halo_block),
            pl.BlockSpec((CONV_WIDTH, D_CONV), const),
            pl.BlockSpec((1, D_CONV), const),
            pl.BlockSpec((1, D_CONV), const),
            pl.BlockSpec((1, D_CONV), const),
            resident((D_MODEL, D_MODEL)),
            pl.BlockSpec((1, D_MODEL), const),
            resident((D_MODEL, D_FF)),
            resident((D_FF, D_MODEL)),
            pl.BlockSpec((1, D_MODEL), const),
        ],
        out_specs=pl.BlockSpec((1, tm, D_MODEL), ffn_tile),
        out_shape=jax.ShapeDtypeStruct((n_tiles, tm, D_MODEL), jnp.float32),
        scratch_shapes=[
            pltpu.VMEM((D_CONV // LANES, CONV_HALO + tm, LANES), jnp.float32),
            pltpu.VMEM((tm, D_CONV), jnp.bfloat16),
        ],
        compiler_params=pltpu.CompilerParams(
            dimension_semantics=("arbitrary",),
            vmem_limit_bytes=VMEM_LIMIT_BYTES),
        name="conv_outproj_ffn",
    )(x, attn, u, u, conv_w, conv_b, ln_g, ln_b, w_o, g2, w1, w2, gf)
    return out.reshape(B, S, D_MODEL)


def kernel(x, norm1_g, w_in, conv_w, conv_b, conv_ln_g, conv_ln_b, w_o, norm2_g, w_ff1, w_ff2,
           rel_bias, final_g):
    assert norm1_g.shape[0] == 1, "single-layer problem"
    bf16 = jnp.bfloat16
    qkv, u = _project(x, norm1_g, w_in[0].astype(bf16))
    attn = _attention(qkv, _bias_tables(rel_bias))
    return _ffn(x, attn, u, conv_w[0], conv_b, conv_ln_g, conv_ln_b, w_o[0].astype(bf16), norm2_g,
                w_ff1[0].astype(bf16), w_ff2[0].astype(bf16), final_g.reshape(1, D_MODEL))
```

```python
import functools
import math

import jax
import jax.numpy as jnp
from jax import lax
from jax.experimental import pallas as pl
from jax.experimental.pallas import tpu as pltpu

D_MODEL = 1024
N_HEADS = 8
HEAD_DIM = 64
D_ATTN = N_HEADS * HEAD_DIM
D_CONV = D_MODEL - D_ATTN
D_QKV = 3 * D_ATTN
D_IN_PROJ = D_QKV + 2 * D_CONV
CONV_WIDTH = 31
D_FF = 4 * D_MODEL
BRANCHES = ((128, 1), (512, 4), (2048, 16))
WIN = 128
NUM_BUCKETS = 32
MAX_DISTANCE = 2048
RMS_EPS = 1e-6
LN_EPS = 1e-5
NEG_INF = -1e30
LOG2E = math.log2(math.e)

LANES = 128
HEADS_PER_GROUP = LANES // HEAD_DIM
N_GROUPS = N_HEADS // HEADS_PER_GROUP
VMEM_LIMIT_BYTES = 56 * 1024 * 1024

PROJ_ROWS = 1024
CONV_HALO = 32
CONV_CHUNK = 32
SUB_DIL = 4
RESIDUE_PITCH = 136
MERGE_ROWS = 256
FFN_ROWS = 512
FF_CHUNK = 512


def _proj_kernel(x_ref, g1_ref, win_ref, qkv_ref, u_ref):
    x = x_ref[0]
    inv = lax.rsqrt(jnp.mean(x * x, axis=-1, keepdims=True) + RMS_EPS)
    h = (x * inv * g1_ref[...]).astype(jnp.bfloat16)

    for g in range(D_CONV // 256):
        a = jnp.dot(h, win_ref[:, D_QKV + g * 256:D_QKV + (g + 1) * 256],
                    preferred_element_type=jnp.float32)
        gate = jnp.dot(h, win_ref[:, D_QKV + D_CONV + g * 256:D_QKV + D_CONV + (g + 1) * 256],
                       preferred_element_type=jnp.float32)
        u = a * jax.nn.sigmoid(gate)
        for half in range(2):
            u_ref[0, 2 * g + half] = u[:, half * LANES:(half + 1) * LANES]

    q_scale = LOG2E / math.sqrt(HEAD_DIM)
    for g in range(D_QKV // 256):
        z = jnp.dot(h, win_ref[:, g * 256:(g + 1) * 256], preferred_element_type=jnp.float32)
        if g * 256 < D_ATTN:
            z = z * q_scale
        qkv_ref[0, :, g * 256:(g + 1) * 256] = z


def _project(x, g1, w_in):
    B, S, _ = x.shape
    tm = PROJ_ROWS
    const = lambda b, i: (0, 0)
    return pl.pallas_call(
        _proj_kernel,
        grid=(B, S // tm),
        in_specs=[
            pl.BlockSpec((1, tm, D_MODEL), lambda b, i: (b, i, 0)),
            pl.BlockSpec((1, D_MODEL), const),
            pl.BlockSpec((D_MODEL, D_IN_PROJ), const),
        ],
        out_specs=[
            pl.BlockSpec((1, tm, D_QKV), lambda b, i: (b, i, 0)),
            pl.BlockSpec((1, D_CONV // LANES, tm, LANES), lambda b, i: (b, 0, i, 0)),
        ],
        out_shape=[
            jax.ShapeDtypeStruct((B, S, D_QKV), jnp.float32),
            jax.ShapeDtypeStruct((B, D_CONV // LANES, S, LANES), jnp.float32),
        ],
        compiler_params=pltpu.CompilerParams(
            dimension_semantics=("parallel", "parallel"),
            vmem_limit_bytes=VMEM_LIMIT_BYTES),
        name="in_proj",
    )(x, g1, w_in)


def _attn_kernel(q_ref, k_ref, v_ref, bias_ref, out_ref, sub_ref, o_ref, l_ref, m_ref):
    assert HEADS_PER_GROUP == 2
    S = out_ref.shape[1]
    nat_refs = (q_ref.at[0], k_ref.at[0], v_ref.at[0])
    sub_len = S // SUB_DIL

    for a in range(3):
        for r in range(SUB_DIL):
            sub_ref[a, r * sub_len:(r + 1) * sub_len, :] = nat_refs[a][pl.ds(r, sub_len, stride=SUB_DIL), :]

    def load_rows(a, dil, res, j0, n):
        if dil == 1:
            x = nat_refs[a][pl.ds(j0, n), :]
        else:
            step = dil // SUB_DIL
            start = (res % SUB_DIL) * sub_len + step * j0 + res // SUB_DIL
            x = sub_ref[a, pl.ds(start, n), :] if step == 1 else sub_ref[a, pl.ds(start, n, stride=step), :]
        return x.astype(jnp.bfloat16)

    lane = lax.broadcasted_iota(jnp.int32, (1, LANES), 1)
    head_lanes = [(lane >= h * HEAD_DIM) & (lane < (h + 1) * HEAD_DIM) for h in range(HEADS_PER_GROUP)]
    head_bits = [jnp.where(hl, jnp.uint32(0xFFFFFFFF), jnp.uint32(0)) for hl in head_lanes]
    head_ones = [jnp.where(hl, 1.0, 0.0).astype(jnp.bfloat16) for hl in head_lanes]

    def keep_head(x, h):
        return pltpu.bitcast(pltpu.bitcast(x, jnp.uint32) & head_bits[h], jnp.bfloat16)

    def window_block(br, dil, res, n, dst_start, dst_stride):
        first = n == 0
        nk = WIN if first else 2 * WIN
        k0 = n * WIN if first else (n - 1) * WIN
        q2 = load_rows(0, dil, res, n * WIN, WIN)
        k2 = load_rows(1, dil, res, k0, nk)
        v2 = load_rows(2, dil, res, k0, nk)
        q_both = jnp.concatenate([keep_head(q2, h) for h in range(HEADS_PER_GROUP)], axis=0)
        s = lax.dot_general(q_both, k2, (((1,), (1,)), ((), ())), preferred_element_type=jnp.float32)
        s = s + bias_ref[br, :, :, 2 * WIN - nk:2 * WIN].reshape(HEADS_PER_GROUP * WIN, nk)
        m = jnp.max(s, axis=-1, keepdims=True)
        p = jnp.exp2(s - m).astype(jnp.bfloat16)
        p_cat = jnp.concatenate([p[h * WIN:(h + 1) * WIN] for h in range(HEADS_PER_GROUP)], axis=1)
        v_cat = jnp.concatenate(
            [jnp.concatenate([keep_head(v2, h), jnp.broadcast_to(head_ones[h], (nk, LANES))], axis=1)
             for h in range(HEADS_PER_GROUP)], axis=0)
        r = jnp.dot(p_cat, v_cat, preferred_element_type=jnp.float32)
        o_ref[br, rows(dst_start, dst_stride), :] = r[:, :LANES]
        l_ref[br, rows(dst_start, dst_stride), :] = r[:, LANES:]
        m_ref[br, rows(dst_start, dst_stride), :] = jnp.where(head_lanes[0], m[:WIN], m[WIN:])

    def rows(start, stride):
        return pl.ds(start, WIN) if stride == 1 else pl.ds(start, WIN, stride=stride)

    gathered = []
    for br, (_, dil) in enumerate(BRANCHES):
        assert dil == 1 or dil % SUB_DIL == 0
        nb = S // dil // WIN
        gathered.append(dil % 8 == 0)
        assert not gathered[br] or nb == 1
        for first in (True, False):
            for res in range(dil):
                for n in range(0 if first else 1, 1 if first else nb):
                    dst = (res * RESIDUE_PITCH, 1) if gathered[br] else (res + dil * WIN * n, dil)
                    window_block(br, dil, res, n, *dst)

    def merge(c, carry):
        def branch_rows(ref, br):
            if not gathered[br]:
                return ref[br, pl.ds(pl.multiple_of(c * MERGE_ROWS, MERGE_ROWS), MERGE_ROWS), :]
            dil = BRANCHES[br][1]
            j0 = c * (MERGE_ROWS // dil)
            return jnp.concatenate(
                [ref[br, pl.ds(j0 + i, dil, stride=RESIDUE_PITCH), :] for i in range(MERGE_ROWS // dil)],
                axis=0)

        ms = [branch_rows(m_ref, br) for br in range(3)]
        m_all = jnp.maximum(jnp.maximum(ms[0], ms[1]), ms[2])
        ws = [jnp.exp2(mb - m_all) for mb in ms]
        num = sum(ws[br] * branch_rows(o_ref, br) for br in range(3))
        den = sum(ws[br] * branch_rows(l_ref, br) for br in range(3))
        out_rows = pl.ds(pl.multiple_of(c * MERGE_ROWS, MERGE_ROWS), MERGE_ROWS)
        out_ref[0, out_rows, :] = (num / den).astype(out_ref.dtype)
        return carry

    lax.fori_loop(0, S // MERGE_ROWS, merge, 0)


def _attention(qkv, bias):
    B, S, _ = qkv.shape
    qkv_specs = [pl.BlockSpec((1, S, LANES), lambda b, p, part=part: (b, 0, part * N_GROUPS + p))
                 for part in range(3)]
    return pl.pallas_call(
        _attn_kernel,
        grid=(B, N_GROUPS),
        in_specs=qkv_specs + [
            pl.BlockSpec((3, HEADS_PER_GROUP, WIN, 2 * WIN), lambda b, p: (0, p, 0, 0)),
        ],
        out_specs=pl.BlockSpec((1, S, LANES), lambda b, p: (b, 0, p)),
        out_shape=jax.ShapeDtypeStruct((B, S, D_ATTN), jnp.bfloat16),
        scratch_shapes=[pltpu.VMEM((3, S, LANES), jnp.float32)] +
                       [pltpu.VMEM((3, max(S, 16 * RESIDUE_PITCH), LANES), jnp.float32)] * 3,
        compiler_params=pltpu.CompilerParams(
            dimension_semantics=("parallel", "parallel"),
            vmem_limit_bytes=VMEM_LIMIT_BYTES),
        name="dilated_attn",
    )(qkv, qkv, qkv, bias)


def _t5_causal_bucket(distance):
    max_exact = NUM_BUCKETS // 2
    d = jnp.maximum(distance, 1).astype(jnp.float32)
    large = max_exact + (jnp.log(d / max_exact) / math.log(MAX_DISTANCE / max_exact)
                         * (NUM_BUCKETS - max_exact)).astype(jnp.int32)
    large = jnp.minimum(large, NUM_BUCKETS - 1)
    return jnp.where(distance < max_exact, distance, large)


def _bias_tables(rel_bias):
    qi = jnp.arange(WIN, dtype=jnp.int32)[:, None]
    kc = jnp.arange(2 * WIN, dtype=jnp.int32)[None, :]
    dist = qi + WIN - kc
    valid = (dist >= 0) & (dist <= WIN)
    tables = []
    for _, dil in BRANCHES:
        bucket = _t5_causal_bucket(jnp.clip(dist, 0, WIN) * dil)
        bias = jnp.zeros((N_HEADS, WIN, 2 * WIN), jnp.float32)
        for b in range(NUM_BUCKETS):
            bias = jnp.where(bucket[None] == b, rel_bias[b].astype(jnp.float32)[:, None, None], bias)
        tables.append(jnp.where(valid[None], bias * LOG2E, NEG_INF))
    return jnp.stack(tables)


def _ffn_kernel(x_ref, attn_ref, u_ref, uh_ref, cw_ref, cb_ref, lng_ref, lnb_ref,
                wo_ref, g2_ref, w1_ref, w2_ref, gf_ref, out_ref, ubuf_ref, conv_ref, *, tiles_per_seq):
    tm = FFN_ROWS
    step = pl.program_id(0)
    n_slabs = D_CONV // LANES

    @pl.when(step == 0)
    def _():
        conv_ref[...] = jnp.zeros_like(conv_ref)

    mixed = jnp.concatenate([attn_ref[0], conv_ref[...]], axis=-1)
    x1 = x_ref[0] + jnp.dot(mixed, wo_ref[...], preferred_element_type=jnp.float32)
    inv = lax.rsqrt(jnp.mean(x1 * x1, axis=-1, keepdims=True) + RMS_EPS)
    h = (x1 * inv * g2_ref[...]).astype(jnp.bfloat16)

    seq_start = (step % tiles_per_seq) == 0
    ubuf_ref[:, 0:CONV_HALO, :] = jnp.where(seq_start, 0.0, uh_ref[0])
    ubuf_ref[:, CONV_HALO:CONV_HALO + tm, :] = u_ref[0]
    off = CONV_HALO - (CONV_WIDTH - 1)

    def exact_zero(tile):
        half_word = jnp.uint32(16)
        bits = lax.shift_right_logical(pltpu.bitcast(tile, jnp.uint32), half_word)
        return pltpu.bitcast(lax.shift_right_logical(bits, half_word), jnp.float32)

    def conv_chunk(c):
        base = c * CONV_CHUNK
        slabs = []
        for s in range(n_slabs):
            cols = slice(s * LANES, (s + 1) * LANES)
            a = jnp.zeros((CONV_CHUNK, LANES), jnp.float32) + cb_ref[:, cols]
            for j in range(CONV_WIDTH):
                tap = ubuf_ref[s, pl.ds(base + off + j, CONV_CHUNK, stride=1), :]
                a = a + tap * cw_ref[j:j + 1, cols]
            slabs.append(a)
        y = jnp.concatenate(slabs, axis=-1)
        mu = jnp.mean(y, axis=-1, keepdims=True)
        cen = y - mu
        var = jnp.mean(cen * cen, axis=-1, keepdims=True)
        y = cen * lax.rsqrt(var + LN_EPS) * lng_ref[...] + lnb_ref[...]
        y = y * jax.nn.sigmoid(y)
        conv_ref[base:base + CONV_CHUNK, :] = y.astype(jnp.bfloat16)
        fold = y[0:8, 0:LANES]
        for r in range(CONV_CHUNK // 8):
            for s in range(n_slabs):
                if r or s:
                    fold = jnp.maximum(fold, y[8 * r:8 * r + 8, s * LANES:(s + 1) * LANES])
        return exact_zero(fold)

    n_ff = D_FF // FF_CHUNK
    chunk_ids = iter(range(tm // CONV_CHUNK))
    anchor_cols = FF_CHUNK * n_ff // (tm // CONV_CHUNK)
    assert anchor_cols % LANES == 0

    def anchored(d):
        top = []
        for g in range(d.shape[1] // anchor_cols):
            top.append(d[0:8, anchor_cols * g:anchor_cols * g + LANES] + conv_chunk(next(chunk_ids)))
            if anchor_cols > LANES:
                top.append(d[0:8, anchor_cols * g + LANES:anchor_cols * (g + 1)])
        return jnp.concatenate([jnp.concatenate(top, axis=1), d[8:]], axis=0)

    acc = x1
    for c in range(n_ff):
        cols = slice(c * FF_CHUNK, (c + 1) * FF_CHUNK)
        f = jnp.maximum(anchored(jnp.dot(h, w1_ref[:, cols], preferred_element_type=jnp.float32)), 0.0)
        f = (f * f).astype(jnp.bfloat16)
        acc = acc + jnp.dot(f, w2_ref[cols, :], preferred_element_type=jnp.float32)
    inv = lax.rsqrt(jnp.mean(acc * acc, axis=-1, keepdims=True) + RMS_EPS)
    out_ref[0] = acc * inv * gf_ref[...]


def _ffn(x, attn, u, conv_w, conv_b, ln_g, ln_b, w_o, g2, w1, w2, gf):
    B, S, _ = x.shape
    tm = FFN_ROWS
    tiles_per_seq = S // tm
    n_tiles = B * tiles_per_seq
    halo_blocks = tm // CONV_HALO
    x = x.reshape(n_tiles, tm, D_MODEL)
    attn = attn.reshape(n_tiles, tm, D_ATTN)

    def ffn_tile(s):
        return (jnp.maximum(s - 1, 0), 0, 0)

    def conv_tile(s):
        t = jnp.minimum(s, n_tiles - 1)
        return t // tiles_per_seq, t % tiles_per_seq

    def u_block(s):
        b, i = conv_tile(s)
        return (b, 0, i, 0)

    def halo_block(s):
        b, i = conv_tile(s)
        return (b, 0, jnp.maximum(i * halo_blocks - 1, 0), 0)

    const = lambda s: (0, 0)
    resident = functools.partial(pl.BlockSpec, index_map=const, pipeline_mode=pl.Buffered(1))
    out = pl.pallas_call(
        functools.partial(_ffn_kernel, tiles_per_seq=tiles_per_seq),
        grid=(n_tiles + 1,),
        in_specs=[
            pl.BlockSpec((1, tm, D_MODEL), ffn_tile),
            pl.BlockSpec((1, tm, D_ATTN), ffn_tile),
            pl.BlockSpec((1, D_CONV // LANES, tm, LANES), u_block),
            pl.BlockSpec((1, D_CONV // LANES, CONV_HALO, LANES), halo_block),
            pl.BlockSpec((CONV_WIDTH, D_CONV), const),
            pl.BlockSpec((1, D_CONV), const),
            pl.BlockSpec((1, D_CONV), const),
            pl.BlockSpec((1, D_CONV), const),
            resident((D_MODEL, D_MODEL)),
            pl.BlockSpec((1, D_MODEL), const),
            resident((D_MODEL, D_FF)),
            resident((D_FF, D_MODEL)),
            pl.BlockSpec((1, D_MODEL), const),
        ],
        out_specs=pl.BlockSpec((1, tm, D_MODEL), ffn_tile),
        out_shape=jax.ShapeDtypeStruct((n_tiles, tm, D_MODEL), jnp.float32),
        scratch_shapes=[
            pltpu.VMEM((D_CONV // LANES, CONV_HALO + tm, LANES), jnp.float32),
            pltpu.VMEM((tm, D_CONV), jnp.bfloat16),
        ],
        compiler_params=pltpu.CompilerParams(
            dimension_semantics=("arbitrary",),
            vmem_limit_bytes=VMEM_LIMIT_BYTES),
        name="conv_outproj_ffn",
    )(x, attn, u, u, conv_w, conv_b, ln_g, ln_b, w_o, g2, w1, w2, gf)
    return out.reshape(B, S, D_MODEL)


def kernel(x, norm1_g, w_in, conv_w, conv_b, conv_ln_g, conv_ln_b, w_o, norm2_g, w_ff1, w_ff2,
           rel_bias, final_g):
    assert norm1_g.shape[0] == 1, "single-layer problem"
    bf16 = jnp.bfloat16
    qkv, u = _project(x, norm1_g, w_in[0].astype(bf16))
    attn = _attention(qkv, _bias_tables(rel_bias))
    return _ffn(x, attn, u, conv_w[0], conv_b, conv_ln_g, conv_ln_b, w_o[0].astype(bf16), norm2_g,
                w_ff1[0].astype(bf16), w_ff2[0].astype(bf16), final_g.reshape(1, D_MODEL))
```

```python
import functools
import math

import jax
import jax.numpy as jnp
from jax import lax
from jax.experimental import pallas as pl
from jax.experimental.pallas import tpu as pltpu

D_MODEL = 1024
N_HEADS = 8
HEAD_DIM = 64
D_ATTN = N_HEADS * HEAD_DIM
D_CONV = D_MODEL - D_ATTN
D_QKV = 3 * D_ATTN
D_IN_PROJ = D_QKV + 2 * D_CONV
CONV_WIDTH = 31
D_FF = 4 * D_MODEL
BRANCHES = ((128, 1), (512, 4), (2048, 16))
WIN = 128
NUM_BUCKETS = 32
MAX_DISTANCE = 2048
RMS_EPS = 1e-6
LN_EPS = 1e-5
NEG_INF = -1e30
LOG2E = math.log2(math.e)

LANES = 128
HEADS_PER_GROUP = LANES // HEAD_DIM
N_GROUPS = N_HEADS // HEADS_PER_GROUP
VMEM_LIMIT_BYTES = 56 * 1024 * 1024

PROJ_ROWS = 1024
CONV_HALO = 32
CONV_CHUNK = 32
SUB_DIL = 4
RESIDUE_PITCH = 136
MERGE_ROWS = 256
BIAS_ROW = 4 * WIN
FFN_ROWS = 512
FF_CHUNK = 512


def _proj_kernel(x_ref, g1_ref, win_ref, qkv_ref, u_ref):
    x = x_ref[0]
    inv = lax.rsqrt(jnp.mean(x * x, axis=-1, keepdims=True) + RMS_EPS)
    h = (x * inv * g1_ref[...]).astype(jnp.bfloat16)

    for g in range(D_CONV // 256):
        a = jnp.dot(h, win_ref[:, D_QKV + g * 256:D_QKV + (g + 1) * 256],
                    preferred_element_type=jnp.float32)
        gate = jnp.dot(h, win_ref[:, D_QKV + D_CONV + g * 256:D_QKV + D_CONV + (g + 1) * 256],
                       preferred_element_type=jnp.float32)
        u = a * jax.nn.sigmoid(gate)
        for half in range(2):
            u_ref[0, 2 * g + half] = u[:, half * LANES:(half + 1) * LANES]

    q_scale = LOG2E / math.sqrt(HEAD_DIM)
    for g in range(D_QKV // 256):
        z = jnp.dot(h, win_ref[:, g * 256:(g + 1) * 256], preferred_element_type=jnp.float32)
        if g * 256 < D_ATTN:
            z = z * q_scale
        qkv_ref[0, :, g * 256:(g + 1) * 256] = z


def _project(x, g1, w_in):
    B, S, _ = x.shape
    tm = PROJ_ROWS
    const = lambda b, i: (0, 0)
    return pl.pallas_call(
        _proj_kernel,
        grid=(B, S // tm),
        in_specs=[
            pl.BlockSpec((1, tm, D_MODEL), lambda b, i: (b, i, 0)),
            pl.BlockSpec((1, D_MODEL), const),
            pl.BlockSpec((D_MODEL, D_IN_PROJ), const),
        ],
        out_specs=[
            pl.BlockSpec((1, tm, D_QKV), lambda b, i: (b, i, 0)),
            pl.BlockSpec((1, D_CONV // LANES, tm, LANES), lambda b, i: (b, 0, i, 0)),
        ],
        out_shape=[
            jax.ShapeDtypeStruct((B, S, D_QKV), jnp.float32),
            jax.ShapeDtypeStruct((B, D_CONV // LANES, S, LANES), jnp.float32),
        ],
        compiler_params=pltpu.CompilerParams(
            dimension_semantics=("parallel", "parallel"),
            vmem_limit_bytes=VMEM_LIMIT_BYTES),
        name="in_proj",
    )(x, g1, w_in)


def _attn_kernel(q_ref, k_ref, v_ref, bias_rows_ref, out_ref, bias_ref, sub_ref, o_ref, l_ref, m_ref):
    assert HEADS_PER_GROUP == 2
    S = out_ref.shape[1]
    first_head = pl.program_id(1) * HEADS_PER_GROUP

    @pl.when((pl.program_id(0) == 0) & (pl.program_id(1) == 0))
    def _():
        for br in range(len(BRANCHES)):
            for h in range(N_HEADS):
                row = jnp.broadcast_to(bias_rows_ref[br, h:h + 1, :], (WIN, BIAS_ROW))
                table = pltpu.roll(row, BIAS_ROW - WIN, 1, stride=1, stride_axis=0)
                bias_ref[br, h] = table[:, :2 * WIN]
    nat_refs = (q_ref.at[0], k_ref.at[0], v_ref.at[0])
    sub_len = S // SUB_DIL

    for a in range(3):
        for r in range(SUB_DIL):
            sub_ref[a, r * sub_len:(r + 1) * sub_len, :] = nat_refs[a][pl.ds(r, sub_len, stride=SUB_DIL), :]

    def load_rows(a, dil, res, j0, n):
        if dil == 1:
            x = nat_refs[a][pl.ds(j0, n), :]
        else:
            step = dil // SUB_DIL
            start = (res % SUB_DIL) * sub_len + step * j0 + res // SUB_DIL
            x = sub_ref[a, pl.ds(start, n), :] if step == 1 else sub_ref[a, pl.ds(start, n, stride=step), :]
        return x.astype(jnp.bfloat16)

    lane = lax.broadcasted_iota(jnp.int32, (1, LANES), 1)
    head_lanes = [(lane >= h * HEAD_DIM) & (lane < (h + 1) * HEAD_DIM) for h in range(HEADS_PER_GROUP)]
    head_bits = [jnp.where(hl, jnp.uint32(0xFFFFFFFF), jnp.uint32(0)) for hl in head_lanes]
    head_ones = [jnp.where(hl, 1.0, 0.0).astype(jnp.bfloat16) for hl in head_lanes]

    def keep_head(x, h):
        return pltpu.bitcast(pltpu.bitcast(x, jnp.uint32) & head_bits[h], jnp.bfloat16)

    def window_block(br, dil, res, n, dst_start, dst_stride):
        first = n == 0
        nk = WIN if first else 2 * WIN
        k0 = n * WIN if first else (n - 1) * WIN
        q2 = load_rows(0, dil, res, n * WIN, WIN)
        k2 = load_rows(1, dil, res, k0, nk)
        v2 = load_rows(2, dil, res, k0, nk)
        q_both = jnp.concatenate([keep_head(q2, h) for h in range(HEADS_PER_GROUP)], axis=0)
        s = lax.dot_general(q_both, k2, (((1,), (1,)), ((), ())), preferred_element_type=jnp.float32)
        bias = bias_ref[br, pl.ds(first_head, HEADS_PER_GROUP), :, 2 * WIN - nk:2 * WIN]
        s = s + bias.reshape(HEADS_PER_GROUP * WIN, nk)
        m = jnp.max(s, axis=-1, keepdims=True)
        p = jnp.exp2(s - m).astype(jnp.bfloat16)
        p_cat = jnp.concatenate([p[h * WIN:(h + 1) * WIN] for h in range(HEADS_PER_GROUP)], axis=1)
        v_cat = jnp.concatenate(
            [jnp.concatenate([keep_head(v2, h), jnp.broadcast_to(head_ones[h], (nk, LANES))], axis=1)
             for h in range(HEADS_PER_GROUP)], axis=0)
        r = jnp.dot(p_cat, v_cat, preferred_element_type=jnp.float32)
        o_ref[br, rows(dst_start, dst_stride), :] = r[:, :LANES]
        l_ref[br, rows(dst_start, dst_stride), :] = r[:, LANES:]
        m_ref[br, rows(dst_start, dst_stride), :] = jnp.where(head_lanes[0], m[:WIN], m[WIN:])

    def rows(start, stride):
        return pl.ds(start, WIN) if stride == 1 else pl.ds(start, WIN, stride=stride)

    gathered = []
    for br, (_, dil) in enumerate(BRANCHES):
        assert dil == 1 or dil % SUB_DIL == 0
        nb = S // dil // WIN
        gathered.append(dil % 8 == 0)
        assert not gathered[br] or nb == 1
        for first in (True, False):
            for res in range(dil):
                for n in range(0 if first else 1, 1 if first else nb):
                    dst = (res * RESIDUE_PITCH, 1) if gathered[br] else (res + dil * WIN * n, dil)
                    window_block(br, dil, res, n, *dst)

    def merge(c, carry):
        def branch_rows(ref, br):
            if not gathered[br]:
                return ref[br, pl.ds(pl.multiple_of(c * MERGE_ROWS, MERGE_ROWS), MERGE_ROWS), :]
            dil = BRANCHES[br][1]
            j0 = c * (MERGE_ROWS // dil)
            return jnp.concatenate(
                [ref[br, pl.ds(j0 + i, dil, stride=RESIDUE_PITCH), :] for i in range(MERGE_ROWS // dil)],
                axis=0)

        ms = [branch_rows(m_ref, br) for br in range(3)]
        m_all = jnp.maximum(jnp.maximum(ms[0], ms[1]), ms[2])
        ws = [jnp.exp2(mb - m_all) for mb in ms]
        num = sum(ws[br] * branch_rows(o_ref, br) for br in range(3))
        den = sum(ws[br] * branch_rows(l_ref, br) for br in range(3))
        out_rows = pl.ds(pl.multiple_of(c * MERGE_ROWS, MERGE_ROWS), MERGE_ROWS)
        out_ref[0, out_rows, :] = (num / den).astype(out_ref.dtype)
        return carry

    lax.fori_loop(0, S // MERGE_ROWS, merge, 0)


def _attention(qkv, bias_rows):
    B, S, _ = qkv.shape
    n_br = len(BRANCHES)
    qkv_specs = [pl.BlockSpec((1, S, LANES), lambda b, p, part=part: (b, 0, part * N_GROUPS + p))
                 for part in range(3)]
    return pl.pallas_call(
        _attn_kernel,
        grid=(B, N_GROUPS),
        in_specs=qkv_specs + [pl.BlockSpec((n_br, N_HEADS, BIAS_ROW), lambda b, p: (0, 0, 0))],
        out_specs=pl.BlockSpec((1, S, LANES), lambda b, p: (b, 0, p)),
        out_shape=jax.ShapeDtypeStruct((B, S, D_ATTN), jnp.bfloat16),
        scratch_shapes=[pltpu.VMEM((n_br, N_HEADS, WIN, 2 * WIN), jnp.float32),
                        pltpu.VMEM((3, S, LANES), jnp.float32)] +
                       [pltpu.VMEM((n_br, max(S, 16 * RESIDUE_PITCH), LANES), jnp.float32)] * 3,
        compiler_params=pltpu.CompilerParams(
            dimension_semantics=("arbitrary", "arbitrary"),
            vmem_limit_bytes=VMEM_LIMIT_BYTES),
        name="dilated_attn",
    )(qkv, qkv, qkv, bias_rows)


def _t5_causal_bucket(distance):
    max_exact = NUM_BUCKETS // 2
    d = jnp.maximum(distance, 1).astype(jnp.float32)
    large = max_exact + (jnp.log(d / max_exact) / math.log(MAX_DISTANCE / max_exact)
                         * (NUM_BUCKETS - max_exact)).astype(jnp.int32)
    large = jnp.minimum(large, NUM_BUCKETS - 1)
    return jnp.where(distance < max_exact, distance, large)


def _bias_rows(rel_bias):
    dist = 2 * WIN - jnp.arange(BIAS_ROW, dtype=jnp.int32)
    valid = (dist >= 0) & (dist <= WIN)
    rows = []
    for _, dil in BRANCHES:
        bucket = _t5_causal_bucket(jnp.clip(dist, 0, WIN) * dil)
        bias = jnp.zeros((N_HEADS, BIAS_ROW), jnp.float32)
        for b in range(NUM_BUCKETS):
            bias = jnp.where(bucket[None] == b, rel_bias[b].astype(jnp.float32)[:, None], bias)
        rows.append(jnp.where(valid[None], bias * LOG2E, NEG_INF))
    return jnp.stack(rows)


def _ffn_kernel(x_ref, attn_ref, u_ref, uh_ref, cw_ref, cb_ref, lng_ref, lnb_ref,
                wo_ref, g2_ref, w1_ref, w2_ref, gf_ref, out_ref, ubuf_ref, conv_ref, *, tiles_per_seq):
    tm = FFN_ROWS
    step = pl.program_id(0)
    n_slabs = D_CONV // LANES

    @pl.when(step == 0)
    def _():
        conv_ref[...] = jnp.zeros_like(conv_ref)

    mixed = jnp.concatenate([attn_ref[0], conv_ref[...]], axis=-1)
    x1 = x_ref[0] + jnp.dot(mixed, wo_ref[...], preferred_element_type=jnp.float32)
    inv = lax.rsqrt(jnp.mean(x1 * x1, axis=-1, keepdims=True) + RMS_EPS)
    h = (x1 * inv * g2_ref[...]).astype(jnp.bfloat16)

    seq_start = (step % tiles_per_seq) == 0
    ubuf_ref[:, 0:CONV_HALO, :] = jnp.where(seq_start, 0.0, uh_ref[0])
    ubuf_ref[:, CONV_HALO:CONV_HALO + tm, :] = u_ref[0]
    off = CONV_HALO - (CONV_WIDTH - 1)

    def exact_zero(tile):
        half_word = jnp.uint32(16)
        bits = lax.shift_right_logical(pltpu.bitcast(tile, jnp.uint32), half_word)
        return pltpu.bitcast(lax.shift_right_logical(bits, half_word), jnp.float32)

    def conv_chunk(c):
        base = c * CONV_CHUNK
        slabs = []
        for s in range(n_slabs):
            cols = slice(s * LANES, (s + 1) * LANES)
            a = jnp.zeros((CONV_CHUNK, LANES), jnp.float32) + cb_ref[:, cols]
            for j in range(CONV_WIDTH):
                tap = ubuf_ref[s, pl.ds(base + off + j, CONV_CHUNK, stride=1), :]
                a = a + tap * cw_ref[j:j + 1, cols]
            slabs.append(a)
        y = jnp.concatenate(slabs, axis=-1)
        mu = jnp.mean(y, axis=-1, keepdims=True)
        cen = y - mu
        var = jnp.mean(cen * cen, axis=-1, keepdims=True)
        y = cen * lax.rsqrt(var + LN_EPS) * lng_ref[...] + lnb_ref[...]
        y = y * jax.nn.sigmoid(y)
        conv_ref[base:base + CONV_CHUNK, :] = y.astype(jnp.bfloat16)
        fold = y[0:8, 0:LANES]
        for r in range(CONV_CHUNK // 8):
            for s in range(n_slabs):
                if r or s:
                    fold = jnp.maximum(fold, y[8 * r:8 * r + 8, s * LANES:(s + 1) * LANES])
        return exact_zero(fold)

    n_ff = D_FF // FF_CHUNK
    chunk_ids = iter(range(tm // CONV_CHUNK))
    anchor_cols = FF_CHUNK * n_ff // (tm // CONV_CHUNK)
    assert anchor_cols % LANES == 0

    def anchored(d):
        top = []
        for g in range(d.shape[1] // anchor_cols):
            top.append(d[0:8, anchor_cols * g:anchor_cols * g + LANES] + conv_chunk(next(chunk_ids)))
            if anchor_cols > LANES:
                top.append(d[0:8, anchor_cols * g + LANES:anchor_cols * (g + 1)])
        return jnp.concatenate([jnp.concatenate(top, axis=1), d[8:]], axis=0)

    acc = x1
    for c in range(n_ff):
        cols = slice(c * FF_CHUNK, (c + 1) * FF_CHUNK)
        f = jnp.maximum(anchored(jnp.dot(h, w1_ref[:, cols], preferred_element_type=jnp.float32)), 0.0)
        f = (f * f).astype(jnp.bfloat16)
        acc = acc + jnp.dot(f, w2_ref[cols, :], preferred_element_type=jnp.float32)
    inv = lax.rsqrt(jnp.mean(acc * acc, axis=-1, keepdims=True) + RMS_EPS)
    out_ref[0] = acc * inv * gf_ref[...]


def _ffn(x, attn, u, conv_w, conv_b, ln_g, ln_b, w_o, g2, w1, w2, gf):
    B, S, _ = x.shape
    tm = FFN_ROWS
    tiles_per_seq = S // tm
    n_tiles = B * tiles_per_seq
    halo_blocks = tm // CONV_HALO
    x = x.reshape(n_tiles, tm, D_MODEL)
    attn = attn.reshape(n_tiles, tm, D_ATTN)

    def ffn_tile(s):
        return (jnp.maximum(s - 1, 0), 0, 0)

    def conv_tile(s):
        t = jnp.minimum(s, n_tiles - 1)
        return t // tiles_per_seq, t % tiles_per_seq

    def u_block(s):
        b, i = conv_tile(s)
        return (b, 0, i, 0)

    def halo_block(s):
        b, i = conv_tile(s)
        return (b, 0, jnp.maximum(i * halo_blocks - 1, 0), 0)

    const = lambda s: (0, 0)
    resident = functools.partial(pl.BlockSpec, index_map=const, pipeline_mode=pl.Buffered(1))
    out = pl.pallas_call(
        functools.partial(_ffn_kernel, tiles_per_seq=tiles_per_seq),
        grid=(n_tiles + 1,),
        in_specs=[
            pl.BlockSpec((1, tm, D_MODEL), ffn_tile),
            pl.BlockSpec((1, tm, D_ATTN), ffn_tile),
            pl.BlockSpec((1, D_CONV // LANES, tm, LANES), u_block),
            pl.BlockSpec((1, D_CONV // LANES, CONV_HALO, LANES), halo_block),
            pl.BlockSpec((CONV_WIDTH, D_CONV), const),
            pl.BlockSpec((1, D_CONV), const),
            pl.BlockSpec((1, D_CONV), const),
            pl.BlockSpec((1, D_CONV), const),
            resident((D_MODEL, D_MODEL)),
            pl.BlockSpec((1, D_MODEL), const),
            resident((D_MODEL, D_FF)),
            resident((D_FF, D_MODEL)),
            pl.BlockSpec((1, D_MODEL), const),
        ],
        out_specs=pl.BlockSpec((1, tm, D_MODEL), ffn_tile),
        out_shape=jax.ShapeDtypeStruct((n_tiles, tm, D_MODEL), jnp.float32),
        scratch_shapes=[
            pltpu.VMEM((D_CONV // LANES, CONV_HALO + tm, LANES), jnp.float32),
            pltpu.VMEM((tm, D_CONV), jnp.bfloat16),
        ],
        compiler_params=pltpu.CompilerParams(
            dimension_semantics=("arbitrary",),
            vmem_limit_bytes=VMEM_LIMIT_BYTES),
        name="conv_outproj_ffn",
    )(x, attn, u, u, conv_w, conv_b, ln_g, ln_b, w_o, g2, w1, w2, gf)
    return out.reshape(B, S, D_MODEL)


def kernel(x, norm1_g, w_in, conv_w, conv_b, conv_ln_g, conv_ln_b, w_o, norm2_g, w_ff1, w_ff2,
           rel_bias, final_g):
    assert norm1_g.shape[0] == 1, "single-layer problem"
    bf16 = jnp.bfloat16
    qkv, u = _project(x, norm1_g, w_in[0].astype(bf16))
    attn = _attention(qkv, _bias_rows(rel_bias))
    return _ffn(x, attn, u, conv_w[0], conv_b, conv_ln_g, conv_ln_b, w_o[0].astype(bf16), norm2_g,
                w_ff1[0].astype(bf16), w_ff2[0].astype(bf16), final_g.reshape(1, D_MODEL))
```

```python
import functools
import math

import jax
import jax.numpy as jnp
from jax import lax
from jax.experimental import pallas as pl
from jax.experimental.pallas import tpu as pltpu

D_MODEL = 1024
N_HEADS = 8
HEAD_DIM = 64
D_ATTN = N_HEADS * HEAD_DIM
D_CONV = D_MODEL - D_ATTN
D_QKV = 3 * D_ATTN
D_IN_PROJ = D_QKV + 2 * D_CONV
CONV_WIDTH = 31
D_FF = 4 * D_MODEL
BRANCHES = ((128, 1), (512, 4), (2048, 16))
WIN = 128
NUM_BUCKETS = 32
MAX_DISTANCE = 2048
RMS_EPS = 1e-6
LN_EPS = 1e-5
NEG_INF = -1e30
LOG2E = math.log2(math.e)

LANES = 128
HEADS_PER_GROUP = LANES // HEAD_DIM
N_GROUPS = N_HEADS // HEADS_PER_GROUP
VMEM_LIMIT_BYTES = 56 * 1024 * 1024

PROJ_ROWS = 1024
CONV_HALO = 32
CONV_CHUNK = 32
SUB_DIL = 4
RESIDUE_PITCH = 136
MERGE_ROWS = 256
BIAS_ROW = 4 * WIN
FFN_ROWS = 512
FF_CHUNK = 512


def _proj_kernel(x_ref, g1_ref, win_ref, wo_ref, w1_ref, w2_ref,
                 qkv_ref, u_ref, wo_bf_ref, w1_bf_ref, w2_bf_ref):
    wo_bf_ref[...] = wo_ref[...].astype(jnp.bfloat16)
    w1_bf_ref[...] = w1_ref[...].astype(jnp.bfloat16)
    w2_bf_ref[...] = w2_ref[...].astype(jnp.bfloat16)

    x = x_ref[0]
    inv = lax.rsqrt(jnp.mean(x * x, axis=-1, keepdims=True) + RMS_EPS)
    h = (x * inv * g1_ref[...]).astype(jnp.bfloat16)

    for g in range(D_CONV // 256):
        a = jnp.dot(h, win_ref[:, D_QKV + g * 256:D_QKV + (g + 1) * 256],
                    preferred_element_type=jnp.float32)
        gate = jnp.dot(h, win_ref[:, D_QKV + D_CONV + g * 256:D_QKV + D_CONV + (g + 1) * 256],
                       preferred_element_type=jnp.float32)
        u = a * jax.nn.sigmoid(gate)
        for half in range(2):
            u_ref[0, 2 * g + half] = u[:, half * LANES:(half + 1) * LANES]

    q_scale = LOG2E / math.sqrt(HEAD_DIM)
    for g in range(D_QKV // 256):
        z = jnp.dot(h, win_ref[:, g * 256:(g + 1) * 256], preferred_element_type=jnp.float32)
        if g * 256 < D_ATTN:
            z = z * q_scale
        qkv_ref[0, :, g * 256:(g + 1) * 256] = z


def _project(x, g1, w_in, w_o, w1, w2):
    B, S, _ = x.shape
    tm = PROJ_ROWS
    tiles_per_seq = S // tm
    n_steps = B * tiles_per_seq
    const = lambda b, i: (0, 0)
    step_rows = lambda b, i: (b * tiles_per_seq + i, 0)

    def weight_slice(w):
        rows = w.shape[0] // n_steps
        assert rows * n_steps == w.shape[0] and rows % 16 == 0
        return pl.BlockSpec((rows, w.shape[1]), step_rows)

    weights = (w_o, w1, w2)
    return pl.pallas_call(
        _proj_kernel,
        grid=(B, tiles_per_seq),
        in_specs=[
            pl.BlockSpec((1, tm, D_MODEL), lambda b, i: (b, i, 0)),
            pl.BlockSpec((1, D_MODEL), const),
            pl.BlockSpec((D_MODEL, D_IN_PROJ), const),
        ] + [weight_slice(w) for w in weights],
        out_specs=[
            pl.BlockSpec((1, tm, D_QKV), lambda b, i: (b, i, 0)),
            pl.BlockSpec((1, D_CONV // LANES, tm, LANES), lambda b, i: (b, 0, i, 0)),
        ] + [weight_slice(w) for w in weights],
        out_shape=[
            jax.ShapeDtypeStruct((B, S, D_QKV), jnp.float32),
            jax.ShapeDtypeStruct((B, D_CONV // LANES, S, LANES), jnp.float32),
        ] + [jax.ShapeDtypeStruct(w.shape, jnp.bfloat16) for w in weights],
        compiler_params=pltpu.CompilerParams(
            dimension_semantics=("parallel", "parallel"),
            vmem_limit_bytes=VMEM_LIMIT_BYTES),
        name="in_proj",
    )(x, g1, w_in, *weights)


def _attn_kernel(q_ref, k_ref, v_ref, bias_rows_ref, out_ref, bias_ref, sub_ref, o_ref, l_ref, m_ref):
    assert HEADS_PER_GROUP == 2
    S = out_ref.shape[1]
    first_head = pl.program_id(1) * HEADS_PER_GROUP

    @pl.when((pl.program_id(0) == 0) & (pl.program_id(1) == 0))
    def _():
        for br in range(len(BRANCHES)):
            for h in range(N_HEADS):
                row = jnp.broadcast_to(bias_rows_ref[br, h:h + 1, :], (WIN, BIAS_ROW))
                table = pltpu.roll(row, BIAS_ROW - WIN, 1, stride=1, stride_axis=0)
                bias_ref[br, h] = table[:, :2 * WIN]
    nat_refs = (q_ref.at[0], k_ref.at[0], v_ref.at[0])
    sub_len = S // SUB_DIL

    for a in range(3):
        for r in range(SUB_DIL):
            sub_ref[a, r * sub_len:(r + 1) * sub_len, :] = nat_refs[a][pl.ds(r, sub_len, stride=SUB_DIL), :]

    def load_rows(a, dil, res, j0, n):
        if dil == 1:
            x = nat_refs[a][pl.ds(j0, n), :]
        else:
            step = dil // SUB_DIL
            start = (res % SUB_DIL) * sub_len + step * j0 + res // SUB_DIL
            x = sub_ref[a, pl.ds(start, n), :] if step == 1 else sub_ref[a, pl.ds(start, n, stride=step), :]
        return x.astype(jnp.bfloat16)

    lane = lax.broadcasted_iota(jnp.int32, (1, LANES), 1)
    head_lanes = [(lane >= h * HEAD_DIM) & (lane < (h + 1) * HEAD_DIM) for h in range(HEADS_PER_GROUP)]
    head_bits = [jnp.where(hl, jnp.uint32(0xFFFFFFFF), jnp.uint32(0)) for hl in head_lanes]
    head_ones = [jnp.where(hl, 1.0, 0.0).astype(jnp.bfloat16) for hl in head_lanes]

    def keep_head(x, h):
        return pltpu.bitcast(pltpu.bitcast(x, jnp.uint32) & head_bits[h], jnp.bfloat16)

    def window_block(br, dil, res, n, dst_start, dst_stride):
        first = n == 0
        nk = WIN if first else 2 * WIN
        k0 = n * WIN if first else (n - 1) * WIN
        q2 = load_rows(0, dil, res, n * WIN, WIN)
        k2 = load_rows(1, dil, res, k0, nk)
        v2 = load_rows(2, dil, res, k0, nk)
        q_both = jnp.concatenate([keep_head(q2, h) for h in range(HEADS_PER_GROUP)], axis=0)
        s = lax.dot_general(q_both, k2, (((1,), (1,)), ((), ())), preferred_element_type=jnp.float32)
        bias = bias_ref[br, pl.ds(first_head, HEADS_PER_GROUP), :, 2 * WIN - nk:2 * WIN]
        s = s + bias.reshape(HEADS_PER_GROUP * WIN, nk)
        m = jnp.max(s, axis=-1, keepdims=True)
        p = jnp.exp2(s - m).astype(jnp.bfloat16)
        p_cat = jnp.concatenate([p[h * WIN:(h + 1) * WIN] for h in range(HEADS_PER_GROUP)], axis=1)
        v_cat = jnp.concatenate(
            [jnp.concatenate([keep_head(v2, h), jnp.broadcast_to(head_ones[h], (nk, LANES))], axis=1)
             for h in range(HEADS_PER_GROUP)], axis=0)
        r = jnp.dot(p_cat, v_cat, preferred_element_type=jnp.float32)
        o_ref[br, rows(dst_start, dst_stride), :] = r[:, :LANES]
        l_ref[br, rows(dst_start, dst_stride), :] = r[:, LANES:]
        m_ref[br, rows(dst_start, dst_stride), :] = jnp.where(head_lanes[0], m[:WIN], m[WIN:])

    def rows(start, stride):
        return pl.ds(start, WIN) if stride == 1 else pl.ds(start, WIN, stride=stride)

    gathered = []
    for br, (_, dil) in enumerate(BRANCHES):
        assert dil == 1 or dil % SUB_DIL == 0
        nb = S // dil // WIN
        gathered.append(dil % 8 == 0)
        assert not gathered[br] or nb == 1
        for first in (True, False):
            for res in range(dil):
                for n in range(0 if first else 1, 1 if first else nb):
                    dst = (res * RESIDUE_PITCH, 1) if gathered[br] else (res + dil * WIN * n, dil)
                    window_block(br, dil, res, n, *dst)

    def merge(c, carry):
        def branch_rows(ref, br):
            if not gathered[br]:
                return ref[br, pl.ds(pl.multiple_of(c * MERGE_ROWS, MERGE_ROWS), MERGE_ROWS), :]
            dil = BRANCHES[br][1]
            j0 = c * (MERGE_ROWS // dil)
            return jnp.concatenate(
                [ref[br, pl.ds(j0 + i, dil, stride=RESIDUE_PITCH), :] for i in range(MERGE_ROWS // dil)],
                axis=0)

        ms = [branch_rows(m_ref, br) for br in range(3)]
        m_all = jnp.maximum(jnp.maximum(ms[0], ms[1]), ms[2])
        ws = [jnp.exp2(mb - m_all) for mb in ms]
        num = sum(ws[br] * branch_rows(o_ref, br) for br in range(3))
        den = sum(ws[br] * branch_rows(l_ref, br) for br in range(3))
        out_rows = pl.ds(pl.multiple_of(c * MERGE_ROWS, MERGE_ROWS), MERGE_ROWS)
        out_ref[0, out_rows, :] = (num / den).astype(out_ref.dtype)
        return carry

    lax.fori_loop(0, S // MERGE_ROWS, merge, 0)


def _attention(qkv, bias_rows):
    B, S, _ = qkv.shape
    n_br = len(BRANCHES)
    qkv_specs = [pl.BlockSpec((1, S, LANES), lambda b, p, part=part: (b, 0, part * N_GROUPS + p))
                 for part in range(3)]
    return pl.pallas_call(
        _attn_kernel,
        grid=(B, N_GROUPS),
        in_specs=qkv_specs + [pl.BlockSpec((n_br, N_HEADS, BIAS_ROW), lambda b, p: (0, 0, 0))],
        out_specs=pl.BlockSpec((1, S, LANES), lambda b, p: (b, 0, p)),
        out_shape=jax.ShapeDtypeStruct((B, S, D_ATTN), jnp.bfloat16),
        scratch_shapes=[pltpu.VMEM((n_br, N_HEADS, WIN, 2 * WIN), jnp.float32),
                        pltpu.VMEM((3, S, LANES), jnp.float32)] +
                       [pltpu.VMEM((n_br, max(S, 16 * RESIDUE_PITCH), LANES), jnp.float32)] * 3,
        compiler_params=pltpu.CompilerParams(
            dimension_semantics=("arbitrary", "arbitrary"),
            vmem_limit_bytes=VMEM_LIMIT_BYTES),
        name="dilated_attn",
    )(qkv, qkv, qkv, bias_rows)


def _t5_causal_bucket(distance):
    max_exact = NUM_BUCKETS // 2
    d = jnp.maximum(distance, 1).astype(jnp.float32)
    large = max_exact + (jnp.log(d / max_exact) / math.log(MAX_DISTANCE / max_exact)
                         * (NUM_BUCKETS - max_exact)).astype(jnp.int32)
    large = jnp.minimum(large, NUM_BUCKETS - 1)
    return jnp.where(distance < max_exact, distance, large)


def _bias_rows(rel_bias):
    dist = 2 * WIN - jnp.arange(BIAS_ROW, dtype=jnp.int32)
    valid = (dist >= 0) & (dist <= WIN)
    rows = []
    for _, dil in BRANCHES:
        bucket = _t5_causal_bucket(jnp.clip(dist, 0, WIN) * dil)
        bias = jnp.zeros((N_HEADS, BIAS_ROW), jnp.float32)
        for b in range(NUM_BUCKETS):
            bias = jnp.where(bucket[None] == b, rel_bias[b].astype(jnp.float32)[:, None], bias)
        rows.append(jnp.where(valid[None], bias * LOG2E, NEG_INF))
    return jnp.stack(rows)


def _ffn_kernel(x_ref, attn_ref, u_ref, uh_ref, cw_ref, cb_ref, lng_ref, lnb_ref,
                wo_ref, g2_ref, w1_ref, w2_ref, gf_ref, out_ref, ubuf_ref, conv_ref, *, tiles_per_seq):
    tm = FFN_ROWS
    step = pl.program_id(0)
    n_slabs = D_CONV // LANES

    @pl.when(step == 0)
    def _():
        conv_ref[...] = jnp.zeros_like(conv_ref)

    mixed = jnp.concatenate([attn_ref[0], conv_ref[...]], axis=-1)
    x1 = x_ref[0] + jnp.dot(mixed, wo_ref[...], preferred_element_type=jnp.float32)
    inv = lax.rsqrt(jnp.mean(x1 * x1, axis=-1, keepdims=True) + RMS_EPS)
    h = (x1 * inv * g2_ref[...]).astype(jnp.bfloat16)

    seq_start = (step % tiles_per_seq) == 0
    ubuf_ref[:, 0:CONV_HALO, :] = jnp.where(seq_start, 0.0, uh_ref[0])
    ubuf_ref[:, CONV_HALO:CONV_HALO + tm, :] = u_ref[0]
    off = CONV_HALO - (CONV_WIDTH - 1)

    def exact_zero(tile):
        half_word = jnp.uint32(16)
        bits = lax.shift_right_logical(pltpu.bitcast(tile, jnp.uint32), half_word)
        return pltpu.bitcast(lax.shift_right_logical(bits, half_word), jnp.float32)

    def conv_chunk(c):
        base = c * CONV_CHUNK
        slabs = []
        for s in range(n_slabs):
            cols = slice(s * LANES, (s + 1) * LANES)
            a = jnp.zeros((CONV_CHUNK, LANES), jnp.float32) + cb_ref[:, cols]
            for j in range(CONV_WIDTH):
                tap = ubuf_ref[s, pl.ds(base + off + j, CONV_CHUNK, stride=1), :]
                a = a + tap * cw_ref[j:j + 1, cols]
            slabs.append(a)
        y = jnp.concatenate(slabs, axis=-1)
        mu = jnp.mean(y, axis=-1, keepdims=True)
        cen = y - mu
        var = jnp.mean(cen * cen, axis=-1, keepdims=True)
        y = cen * lax.rsqrt(var + LN_EPS) * lng_ref[...] + lnb_ref[...]
        y = y * jax.nn.sigmoid(y)
        conv_ref[base:base + CONV_CHUNK, :] = y.astype(jnp.bfloat16)
        fold = y[0:8, 0:LANES]
        for r in range(CONV_CHUNK // 8):
            for s in range(n_slabs):
                if r or s:
                    fold = jnp.maximum(fold, y[8 * r:8 * r + 8, s * LANES:(s + 1) * LANES])
        return exact_zero(fold)

    n_ff = D_FF // FF_CHUNK
    chunk_ids = iter(range(tm // CONV_CHUNK))
    anchor_cols = FF_CHUNK * n_ff // (tm // CONV_CHUNK)
    assert anchor_cols % LANES == 0

    def anchored(d):
        top = []
        for g in range(d.shape[1] // anchor_cols):
            top.append(d[0:8, anchor_cols * g:anchor_cols * g + LANES] + conv_chunk(next(chunk_ids)))
            if anchor_cols > LANES:
                top.append(d[0:8, anchor_cols * g + LANES:anchor_cols * (g + 1)])
        return jnp.concatenate([jnp.concatenate(top, axis=1), d[8:]], axis=0)

    acc = x1
    for c in range(n_ff):
        cols = slice(c * FF_CHUNK, (c + 1) * FF_CHUNK)
        f = jnp.maximum(anchored(jnp.dot(h, w1_ref[:, cols], preferred_element_type=jnp.float32)), 0.0)
        f = (f * f).astype(jnp.bfloat16)
        acc = acc + jnp.dot(f, w2_ref[cols, :], preferred_element_type=jnp.float32)
    inv = lax.rsqrt(jnp.mean(acc * acc, axis=-1, keepdims=True) + RMS_EPS)
    out_ref[0] = acc * inv * gf_ref[...]


def _ffn(x, attn, u, conv_w, conv_b, ln_g, ln_b, w_o, g2, w1, w2, gf):
    B, S, _ = x.shape
    tm = FFN_ROWS
    tiles_per_seq = S // tm
    n_tiles = B * tiles_per_seq
    halo_blocks = tm // CONV_HALO
    x = x.reshape(n_tiles, tm, D_MODEL)
    attn = attn.reshape(n_tiles, tm, D_ATTN)

    def ffn_tile(s):
        return (jnp.maximum(s - 1, 0), 0, 0)

    def conv_tile(s):
        t = jnp.minimum(s, n_tiles - 1)
        return t // tiles_per_seq, t % tiles_per_seq

    def u_block(s):
        b, i = conv_tile(s)
        return (b, 0, i, 0)

    def halo_block(s):
        b, i = conv_tile(s)
        return (b, 0, jnp.maximum(i * halo_blocks - 1, 0), 0)

    const = lambda s: (0, 0)
    resident = functools.partial(pl.BlockSpec, index_map=const, pipeline_mode=pl.Buffered(1))
    out = pl.pallas_call(
        functools.partial(_ffn_kernel, tiles_per_seq=tiles_per_seq),
        grid=(n_tiles + 1,),
        in_specs=[
            pl.BlockSpec((1, tm, D_MODEL), ffn_tile),
            pl.BlockSpec((1, tm, D_ATTN), ffn_tile),
            pl.BlockSpec((1, D_CONV // LANES, tm, LANES), u_block),
            pl.BlockSpec((1, D_CONV // LANES, CONV_HALO, LANES), halo_block),
            pl.BlockSpec((CONV_WIDTH, D_CONV), const),
            pl.BlockSpec((1, D_CONV), const),
            pl.BlockSpec((1, D_CONV), const),
            pl.BlockSpec((1, D_CONV), const),
            resident((D_MODEL, D_MODEL)),
            pl.BlockSpec((1, D_MODEL), const),
            resident((D_MODEL, D_FF)),
            resident((D_FF, D_MODEL)),
            pl.BlockSpec((1, D_MODEL), const),
        ],
        out_specs=pl.BlockSpec((1, tm, D_MODEL), ffn_tile),
        out_shape=jax.ShapeDtypeStruct((n_tiles, tm, D_MODEL), jnp.float32),
        scratch_shapes=[
            pltpu.VMEM((D_CONV // LANES, CONV_HALO + tm, LANES), jnp.float32),
            pltpu.VMEM((tm, D_CONV), jnp.bfloat16),
        ],
        compiler_params=pltpu.CompilerParams(
            dimension_semantics=("arbitrary",),
            vmem_limit_bytes=VMEM_LIMIT_BYTES),
        name="conv_outproj_ffn",
    )(x, attn, u, u, conv_w, conv_b, ln_g, ln_b, w_o, g2, w1, w2, gf)
    return out.reshape(B, S, D_MODEL)


def kernel(x, norm1_g, w_in, conv_w, conv_b, conv_ln_g, conv_ln_b, w_o, norm2_g, w_ff1, w_ff2,
           rel_bias, final_g):
    assert norm1_g.shape[0] == 1, "single-layer problem"
    bf16 = jnp.bfloat16
    qkv, u, w_o_bf, w1_bf, w2_bf = _project(x, norm1_g, w_in[0].astype(bf16), w_o[0], w_ff1[0], w_ff2[0])
    attn = _attention(qkv, _bias_rows(rel_bias))
    return _ffn(x, attn, u, conv_w[0], conv_b, conv_ln_g, conv_ln_b, w_o_bf, norm2_g,
                w1_bf, w2_bf, final_g.reshape(1, D_MODEL))
```

```python
import functools
import math

import jax
import jax.numpy as jnp
from jax import lax
from jax.experimental import pallas as pl
from jax.experimental.pallas import tpu as pltpu

D_MODEL = 1024
N_HEADS = 8
HEAD_DIM = 64
D_ATTN = N_HEADS * HEAD_DIM
D_CONV = D_MODEL - D_ATTN
D_QKV = 3 * D_ATTN
D_IN_PROJ = D_QKV + 2 * D_CONV
CONV_WIDTH = 31
D_FF = 4 * D_MODEL
BRANCHES = ((128, 1), (512, 4), (2048, 16))
WIN = 128
NUM_BUCKETS = 32
MAX_DISTANCE = 2048
RMS_EPS = 1e-6
LN_EPS = 1e-5
NEG_INF = -1e30
LOG2E = math.log2(math.e)

LANES = 128
HEADS_PER_GROUP = LANES // HEAD_DIM
N_GROUPS = N_HEADS // HEADS_PER_GROUP
VMEM_LIMIT_BYTES = 56 * 1024 * 1024

PROJ_ROWS = 1024
CONV_HALO = 32
CONV_CHUNK = 32
SUB_DIL = 4
RESIDUE_PITCH = 136
MERGE_ROWS = 512
BIAS_ROW = 4 * WIN
FFN_ROWS = 512
FF_CHUNK = 512


def _proj_kernel(x_ref, g1_ref, win_ref, wo_ref, w1_ref, w2_ref,
                 qkv_ref, u_ref, wo_bf_ref, w1_bf_ref, w2_bf_ref):
    wo_bf_ref[...] = wo_ref[...].astype(jnp.bfloat16)
    w1_bf_ref[...] = w1_ref[...].astype(jnp.bfloat16)
    w2_bf_ref[...] = w2_ref[...].astype(jnp.bfloat16)

    x = x_ref[0]
    inv = lax.rsqrt(jnp.mean(x * x, axis=-1, keepdims=True) + RMS_EPS)
    h = (x * inv * g1_ref[...]).astype(jnp.bfloat16)

    for g in range(D_CONV // 256):
        a = jnp.dot(h, win_ref[:, D_QKV + g * 256:D_QKV + (g + 1) * 256],
                    preferred_element_type=jnp.float32)
        gate = jnp.dot(h, win_ref[:, D_QKV + D_CONV + g * 256:D_QKV + D_CONV + (g + 1) * 256],
                       preferred_element_type=jnp.float32)
        u = a * jax.nn.sigmoid(gate)
        for half in range(2):
            u_ref[0, 2 * g + half] = u[:, half * LANES:(half + 1) * LANES]

    q_scale = LOG2E / math.sqrt(HEAD_DIM)
    for g in range(D_QKV // 256):
        z = jnp.dot(h, win_ref[:, g * 256:(g + 1) * 256], preferred_element_type=jnp.float32)
        if g * 256 < D_ATTN:
            z = z * q_scale
        qkv_ref[0, :, g * 256:(g + 1) * 256] = z


def _project(x, g1, w_in, w_o, w1, w2):
    B, S, _ = x.shape
    tm = PROJ_ROWS
    tiles_per_seq = S // tm
    n_steps = B * tiles_per_seq
    const = lambda b, i: (0, 0)
    step_rows = lambda b, i: (b * tiles_per_seq + i, 0)

    def weight_slice(w):
        rows = w.shape[0] // n_steps
        assert rows * n_steps == w.shape[0] and rows % 16 == 0
        return pl.BlockSpec((rows, w.shape[1]), step_rows)

    weights = (w_o, w1, w2)
    return pl.pallas_call(
        _proj_kernel,
        grid=(B, tiles_per_seq),
        in_specs=[
            pl.BlockSpec((1, tm, D_MODEL), lambda b, i: (b, i, 0)),
            pl.BlockSpec((1, D_MODEL), const),
            pl.BlockSpec((D_MODEL, D_IN_PROJ), const),
        ] + [weight_slice(w) for w in weights],
        out_specs=[
            pl.BlockSpec((1, tm, D_QKV), lambda b, i: (b, i, 0)),
            pl.BlockSpec((1, D_CONV // LANES, tm, LANES), lambda b, i: (b, 0, i, 0)),
        ] + [weight_slice(w) for w in weights],
        out_shape=[
            jax.ShapeDtypeStruct((B, S, D_QKV), jnp.float32),
            jax.ShapeDtypeStruct((B, D_CONV // LANES, S, LANES), jnp.float32),
        ] + [jax.ShapeDtypeStruct(w.shape, jnp.bfloat16) for w in weights],
        compiler_params=pltpu.CompilerParams(
            dimension_semantics=("parallel", "parallel"),
            vmem_limit_bytes=VMEM_LIMIT_BYTES),
        name="in_proj",
    )(x, g1, w_in, *weights)


def _attn_kernel(q_ref, k_ref, v_ref, bias_rows_ref, out_ref, bias_ref, sub_ref, o_ref, l_ref, m_ref):
    assert HEADS_PER_GROUP == 2
    S = out_ref.shape[1]
    first_head = pl.program_id(1) * HEADS_PER_GROUP

    @pl.when((pl.program_id(0) == 0) & (pl.program_id(1) == 0))
    def _():
        for br in range(len(BRANCHES)):
            for h in range(N_HEADS):
                row = jnp.broadcast_to(bias_rows_ref[br, h:h + 1, :], (WIN, BIAS_ROW))
                table = pltpu.roll(row, BIAS_ROW - WIN, 1, stride=1, stride_axis=0)
                bias_ref[br, h] = table[:, :2 * WIN]
    nat_refs = (q_ref.at[0], k_ref.at[0], v_ref.at[0])
    sub_len = S // SUB_DIL

    for a in range(3):
        for r in range(SUB_DIL):
            sub_ref[a, r * sub_len:(r + 1) * sub_len, :] = nat_refs[a][pl.ds(r, sub_len, stride=SUB_DIL), :]

    def load_rows(a, dil, res, j0, n):
        if dil == 1:
            x = nat_refs[a][pl.ds(j0, n), :]
        else:
            step = dil // SUB_DIL
            start = (res % SUB_DIL) * sub_len + step * j0 + res // SUB_DIL
            x = sub_ref[a, pl.ds(start, n), :] if step == 1 else sub_ref[a, pl.ds(start, n, stride=step), :]
        return x.astype(jnp.bfloat16)

    lane = lax.broadcasted_iota(jnp.int32, (1, LANES), 1)
    head_lanes = [(lane >= h * HEAD_DIM) & (lane < (h + 1) * HEAD_DIM) for h in range(HEADS_PER_GROUP)]
    head_bits = [jnp.where(hl, jnp.uint32(0xFFFFFFFF), jnp.uint32(0)) for hl in head_lanes]
    head_ones = [jnp.where(hl, 1.0, 0.0).astype(jnp.bfloat16) for hl in head_lanes]

    def keep_head(x, h):
        return pltpu.bitcast(pltpu.bitcast(x, jnp.uint32) & head_bits[h], jnp.bfloat16)

    def window_block(br, dil, res, n, dst_start, dst_stride):
        first = n == 0
        nk = WIN if first else 2 * WIN
        k0 = n * WIN if first else (n - 1) * WIN
        q2 = load_rows(0, dil, res, n * WIN, WIN)
        k2 = load_rows(1, dil, res, k0, nk)
        v2 = load_rows(2, dil, res, k0, nk)
        q_both = jnp.concatenate([keep_head(q2, h) for h in range(HEADS_PER_GROUP)], axis=0)
        s = lax.dot_general(q_both, k2, (((1,), (1,)), ((), ())), preferred_element_type=jnp.float32)
        bias = bias_ref[br, pl.ds(first_head, HEADS_PER_GROUP), :, 2 * WIN - nk:2 * WIN]
        s = s + bias.reshape(HEADS_PER_GROUP * WIN, nk)
        m = jnp.max(s, axis=-1, keepdims=True)
        p = jnp.exp2(s - m).astype(jnp.bfloat16)
        p_cat = jnp.concatenate([p[h * WIN:(h + 1) * WIN] for h in range(HEADS_PER_GROUP)], axis=1)
        v_cat = jnp.concatenate(
            [jnp.concatenate([keep_head(v2, h), jnp.broadcast_to(head_ones[h], (nk, LANES))], axis=1)
             for h in range(HEADS_PER_GROUP)], axis=0)
        r = jnp.dot(p_cat, v_cat, preferred_element_type=jnp.float32)
        o_ref[br, rows(dst_start, dst_stride), :] = r[:, :LANES]
        l_ref[br, rows(dst_start, dst_stride), :] = r[:, LANES:]
        m_ref[br, rows(dst_start, dst_stride), :] = jnp.where(head_lanes[0], m[:WIN], m[WIN:])

    def rows(start, stride):
        return pl.ds(start, WIN) if stride == 1 else pl.ds(start, WIN, stride=stride)

    gathered = []
    for br, (_, dil) in enumerate(BRANCHES):
        assert dil == 1 or dil % SUB_DIL == 0
        nb = S // dil // WIN
        gathered.append(dil % 8 == 0)
        assert not gathered[br] or nb == 1
        for first in (True, False):
            for res in range(dil):
                for n in range(0 if first else 1, 1 if first else nb):
                    dst = (res * RESIDUE_PITCH, 1) if gathered[br] else (res + dil * WIN * n, dil)
                    window_block(br, dil, res, n, *dst)

    def merge(c, carry):
        def branch_rows(ref, br):
            if not gathered[br]:
                return ref[br, pl.ds(pl.multiple_of(c * MERGE_ROWS, MERGE_ROWS), MERGE_ROWS), :]
            dil = BRANCHES[br][1]
            j0 = c * (MERGE_ROWS // dil)
            return jnp.concatenate(
                [ref[br, pl.ds(j0 + i, dil, stride=RESIDUE_PITCH), :] for i in range(MERGE_ROWS // dil)],
                axis=0)

        ms = [branch_rows(m_ref, br) for br in range(3)]
        m_all = jnp.maximum(jnp.maximum(ms[0], ms[1]), ms[2])
        ws = [jnp.exp2(mb - m_all) for mb in ms]
        num = sum(ws[br] * branch_rows(o_ref, br) for br in range(3))
        den = sum(ws[br] * branch_rows(l_ref, br) for br in range(3))
        out_rows = pl.ds(pl.multiple_of(c * MERGE_ROWS, MERGE_ROWS), MERGE_ROWS)
        out_ref[0, out_rows, :] = (num / den).astype(out_ref.dtype)
        return carry

    lax.fori_loop(0, S // MERGE_ROWS, merge, 0)


def _attention(qkv, bias_rows):
    B, S, _ = qkv.shape
    n_br = len(BRANCHES)
    qkv_specs = [pl.BlockSpec((1, S, LANES), lambda b, p, part=part: (b, 0, part * N_GROUPS + p))
                 for part in range(3)]
    return pl.pallas_call(
        _attn_kernel,
        grid=(B, N_GROUPS),
        in_specs=qkv_specs + [pl.BlockSpec((n_br, N_HEADS, BIAS_ROW), lambda b, p: (0, 0, 0))],
        out_specs=pl.BlockSpec((1, S, LANES), lambda b, p: (b, 0, p)),
        out_shape=jax.ShapeDtypeStruct((B, S, D_ATTN), jnp.bfloat16),
        scratch_shapes=[pltpu.VMEM((n_br, N_HEADS, WIN, 2 * WIN), jnp.float32),
                        pltpu.VMEM((3, S, LANES), jnp.float32)] +
                       [pltpu.VMEM((n_br, max(S, 16 * RESIDUE_PITCH), LANES), jnp.float32)] * 3,
        compiler_params=pltpu.CompilerParams(
            dimension_semantics=("arbitrary", "arbitrary"),
            vmem_limit_bytes=VMEM_LIMIT_BYTES),
        name="dilated_attn",
    )(qkv, qkv, qkv, bias_rows)


def _t5_causal_bucket(distance):
    max_exact = NUM_BUCKETS // 2
    d = jnp.maximum(distance, 1).astype(jnp.float32)
    large = max_exact + (jnp.log(d / max_exact) / math.log(MAX_DISTANCE / max_exact)
                         * (NUM_BUCKETS - max_exact)).astype(jnp.int32)
    large = jnp.minimum(large, NUM_BUCKETS - 1)
    return jnp.where(distance < max_exact, distance, large)


def _bias_rows(rel_bias):
    dist = 2 * WIN - jnp.arange(BIAS_ROW, dtype=jnp.int32)
    valid = (dist >= 0) & (dist <= WIN)
    rows = []
    for _, dil in BRANCHES:
        bucket = _t5_causal_bucket(jnp.clip(dist, 0, WIN) * dil)
        bias = jnp.zeros((N_HEADS, BIAS_ROW), jnp.float32)
        for b in range(NUM_BUCKETS):
            bias = jnp.where(bucket[None] == b, rel_bias[b].astype(jnp.float32)[:, None], bias)
        rows.append(jnp.where(valid[None], bias * LOG2E, NEG_INF))
    return jnp.stack(rows)


def _ffn_kernel(x_ref, attn_ref, u_ref, uh_ref, cw_ref, cb_ref, lng_ref, lnb_ref,
                wo_ref, g2_ref, w1_ref, w2_ref, gf_ref, out_ref, ubuf_ref, conv_ref, *, tiles_per_seq):
    tm = FFN_ROWS
    step = pl.program_id(0)
    last_step = pl.num_programs(0) - 1
    n_slabs = D_CONV // LANES
    n_ff = D_FF // FF_CHUNK
    n_chunks = tm // CONV_CHUNK
    off = CONV_HALO - (CONV_WIDTH - 1)

    def exact_zero(tile):
        half_word = jnp.uint32(16)
        bits = lax.shift_right_logical(pltpu.bitcast(tile, jnp.uint32), half_word)
        return pltpu.bitcast(lax.shift_right_logical(bits, half_word), jnp.float32)

    def conv_chunk(c):
        base = c * CONV_CHUNK
        slabs = []
        for s in range(n_slabs):
            cols = slice(s * LANES, (s + 1) * LANES)
            a = jnp.zeros((CONV_CHUNK, LANES), jnp.float32) + cb_ref[:, cols]
            for j in range(CONV_WIDTH):
                tap = ubuf_ref[s, pl.ds(base + off + j, CONV_CHUNK, stride=1), :]
                a = a + tap * cw_ref[j:j + 1, cols]
            slabs.append(a)
        y = jnp.concatenate(slabs, axis=-1)
        mu = jnp.mean(y, axis=-1, keepdims=True)
        cen = y - mu
        var = jnp.mean(cen * cen, axis=-1, keepdims=True)
        y = cen * lax.rsqrt(var + LN_EPS) * lng_ref[...] + lnb_ref[...]
        y = y * jax.nn.sigmoid(y)
        conv_ref[base:base + CONV_CHUNK, :] = y.astype(jnp.bfloat16)
        fold = y[0:8, 0:LANES]
        for r in range(CONV_CHUNK // 8):
            for s in range(n_slabs):
                if r or s:
                    fold = jnp.maximum(fold, y[8 * r:8 * r + 8, s * LANES:(s + 1) * LANES])
        return exact_zero(fold)

    def step_body(ffn_half, conv_half):
        if ffn_half:
            mixed = jnp.concatenate([attn_ref[0], conv_ref[...]], axis=-1)
            x1 = x_ref[0] + jnp.dot(mixed, wo_ref[...], preferred_element_type=jnp.float32)
            inv = lax.rsqrt(jnp.mean(x1 * x1, axis=-1, keepdims=True) + RMS_EPS)
            h = (x1 * inv * g2_ref[...]).astype(jnp.bfloat16)
        if conv_half:
            seq_start = (step % tiles_per_seq) == 0
            ubuf_ref[:, 0:CONV_HALO, :] = jnp.where(seq_start, 0.0, uh_ref[0])
            ubuf_ref[:, CONV_HALO:CONV_HALO + tm, :] = u_ref[0]
        if not ffn_half:
            for c in range(n_chunks):
                conv_chunk(c)
            return

        chunk_ids = iter(range(n_chunks))
        anchor_cols = FF_CHUNK * n_ff // n_chunks
        assert anchor_cols % LANES == 0

        def anchored(d):
            top = []
            for g in range(d.shape[1] // anchor_cols):
                top.append(d[0:8, anchor_cols * g:anchor_cols * g + LANES] + conv_chunk(next(chunk_ids)))
                if anchor_cols > LANES:
                    top.append(d[0:8, anchor_cols * g + LANES:anchor_cols * (g + 1)])
            return jnp.concatenate([jnp.concatenate(top, axis=1), d[8:]], axis=0)

        acc = x1
        for c in range(n_ff):
            cols = slice(c * FF_CHUNK, (c + 1) * FF_CHUNK)
            d = jnp.dot(h, w1_ref[:, cols], preferred_element_type=jnp.float32)
            f = jnp.maximum(anchored(d) if conv_half else d, 0.0)
            f = (f * f).astype(jnp.bfloat16)
            acc = acc + jnp.dot(f, w2_ref[cols, :], preferred_element_type=jnp.float32)
        inv = lax.rsqrt(jnp.mean(acc * acc, axis=-1, keepdims=True) + RMS_EPS)
        out_ref[0] = acc * inv * gf_ref[...]

    pl.when(step == 0)(lambda: step_body(False, True))
    pl.when((step > 0) & (step < last_step))(lambda: step_body(True, True))
    pl.when(step == last_step)(lambda: step_body(True, False))


def _ffn(x, attn, u, conv_w, conv_b, ln_g, ln_b, w_o, g2, w1, w2, gf):
    B, S, _ = x.shape
    tm = FFN_ROWS
    tiles_per_seq = S // tm
    n_tiles = B * tiles_per_seq
    halo_blocks = tm // CONV_HALO
    x = x.reshape(n_tiles, tm, D_MODEL)
    attn = attn.reshape(n_tiles, tm, D_ATTN)

    def ffn_tile(s):
        return (jnp.maximum(s - 1, 0), 0, 0)

    def conv_tile(s):
        t = jnp.minimum(s, n_tiles - 1)
        return t // tiles_per_seq, t % tiles_per_seq

    def u_block(s):
        b, i = conv_tile(s)
        return (b, 0, i, 0)

    def halo_block(s):
        b, i = conv_tile(s)
        return (b, 0, jnp.maximum(i * halo_blocks - 1, 0), 0)

    const = lambda s: (0, 0)
    resident = functools.partial(pl.BlockSpec, index_map=const, pipeline_mode=pl.Buffered(1))
    out = pl.pallas_call(
        functools.partial(_ffn_kernel, tiles_per_seq=tiles_per_seq),
        grid=(n_tiles + 1,),
        in_specs=[
            pl.BlockSpec((1, tm, D_MODEL), ffn_tile),
            pl.BlockSpec((1, tm, D_ATTN), ffn_tile),
            pl.BlockSpec((1, D_CONV // LANES, tm, LANES), u_block),
            pl.BlockSpec((1, D_CONV // LANES, CONV_HALO, LANES), halo_block),
            pl.BlockSpec((CONV_WIDTH, D_CONV), const),
            pl.BlockSpec((1, D_CONV), const),
            pl.BlockSpec((1, D_CONV), const),
            pl.BlockSpec((1, D_CONV), const),
            resident((D_MODEL, D_MODEL)),
            pl.BlockSpec((1, D_MODEL), const),
            resident((D_MODEL, D_FF)),
            resident((D_FF, D_MODEL)),
            pl.BlockSpec((1, D_MODEL), const),
        ],
        out_specs=pl.BlockSpec((1, tm, D_MODEL), ffn_tile),
        out_shape=jax.ShapeDtypeStruct((n_tiles, tm, D_MODEL), jnp.float32),
        scratch_shapes=[
            pltpu.VMEM((D_CONV // LANES, CONV_HALO + tm, LANES), jnp.float32),
            pltpu.VMEM((tm, D_CONV), jnp.bfloat16),
        ],
        compiler_params=pltpu.CompilerParams(
            dimension_semantics=("arbitrary",),
            vmem_limit_bytes=VMEM_LIMIT_BYTES),
        name="conv_outproj_ffn",
    )(x, attn, u, u, conv_w, conv_b, ln_g, ln_b, w_o, g2, w1, w2, gf)
    return out.reshape(B, S, D_MODEL)


def kernel(x, norm1_g, w_in, conv_w, conv_b, conv_ln_g, conv_ln_b, w_o, norm2_g, w_ff1, w_ff2,
           rel_bias, final_g):
    assert norm1_g.shape[0] == 1, "single-layer problem"
    bf16 = jnp.bfloat16
    qkv, u, w_o_bf, w1_bf, w2_bf = _project(x, norm1_g, w_in[0].astype(bf16), w_o[0], w_ff1[0], w_ff2[0])
    attn = _attention(qkv, _bias_rows(rel_bias))
    return _ffn(x, attn, u, conv_w[0], conv_b, conv_ln_g, conv_ln_b, w_o_bf, norm2_g,
                w1_bf, w2_bf, final_g.reshape(1, D_MODEL))
```

```python
import functools
import math

import jax
import jax.numpy as jnp
from jax import lax
from jax.experimental import pallas as pl
from jax.experimental.pallas import tpu as pltpu

D_MODEL = 1024
N_HEADS = 8
HEAD_DIM = 64
D_ATTN = N_HEADS * HEAD_DIM
D_CONV = D_MODEL - D_ATTN
D_QKV = 3 * D_ATTN
D_IN_PROJ = D_QKV + 2 * D_CONV
CONV_WIDTH = 31
D_FF = 4 * D_MODEL
BRANCHES = ((128, 1), (512, 4), (2048, 16))
WIN = 128
NUM_BUCKETS = 32
MAX_DISTANCE = 2048
RMS_EPS = 1e-6
LN_EPS = 1e-5
NEG_INF = -1e30
LOG2E = math.log2(math.e)

LANES = 128
BF16_TILE_ROWS = 16
PROJ_COLS = 256
HEADS_PER_GROUP = LANES // HEAD_DIM
N_GROUPS = N_HEADS // HEADS_PER_GROUP
VMEM_LIMIT_BYTES = 56 * 1024 * 1024

PROJ_ROWS = 1024
CONV_HALO = 32
CONV_CHUNK = 32
SUB_DIL = 4
RESIDUE_PITCH = 136
MERGE_ROWS = 512
BIAS_ROW = 4 * WIN
FFN_ROWS = 512
FF_CHUNK = 512


def _proj_kernel(x_ref, g1_ref, win_f32_ref, wo_ref, w1_ref, w2_ref,
                 qkv_ref, u_ref, wo_bf_ref, w1_bf_ref, w2_bf_ref, win_ref):
    @pl.when((pl.program_id(0) == 0) & (pl.program_id(1) == 0))
    def _():
        win_ref[...] = win_f32_ref[...].astype(jnp.bfloat16)

    wo_bf_ref[...] = wo_ref[...].astype(jnp.bfloat16)
    w1_bf_ref[...] = w1_ref[...].astype(jnp.bfloat16)
    w2_bf_ref[...] = w2_ref[...].astype(jnp.bfloat16)

    x = x_ref[0]
    inv = lax.rsqrt(jnp.mean(x * x, axis=-1, keepdims=True) + RMS_EPS)
    h = (x * inv * g1_ref[...]).astype(jnp.bfloat16)

    slabs_per_group = PROJ_COLS // LANES
    for g in range(D_CONV // PROJ_COLS):
        a_cols = slice(D_QKV + g * PROJ_COLS, D_QKV + (g + 1) * PROJ_COLS)
        gate_cols = slice(D_QKV + D_CONV + g * PROJ_COLS, D_QKV + D_CONV + (g + 1) * PROJ_COLS)
        a = jnp.dot(h, win_ref[:, a_cols], preferred_element_type=jnp.float32)
        gate = jnp.dot(h, win_ref[:, gate_cols], preferred_element_type=jnp.float32)
        u = a * jax.nn.sigmoid(gate)
        for k in range(slabs_per_group):
            u_ref[0, slabs_per_group * g + k] = u[:, k * LANES:(k + 1) * LANES]

    q_scale = LOG2E / math.sqrt(HEAD_DIM)
    for g in range(D_QKV // PROJ_COLS):
        cols = slice(g * PROJ_COLS, (g + 1) * PROJ_COLS)
        z = jnp.dot(h, win_ref[:, cols], preferred_element_type=jnp.float32)
        if g * PROJ_COLS < D_ATTN:
            z = z * q_scale
        qkv_ref[0, :, cols] = z


def _project(x, g1, w_in, w_o, w1, w2):
    B, S, _ = x.shape
    tm = PROJ_ROWS
    tiles_per_seq = S // tm
    n_steps = B * tiles_per_seq
    const = lambda b, i: (0, 0)
    step_rows = lambda b, i: (b * tiles_per_seq + i, 0)

    def weight_slice(w):
        rows = w.shape[0] // n_steps
        assert rows * n_steps == w.shape[0] and rows % BF16_TILE_ROWS == 0
        return pl.BlockSpec((rows, w.shape[1]), step_rows)

    weights = (w_o, w1, w2)
    return pl.pallas_call(
        _proj_kernel,
        grid=(B, tiles_per_seq),
        in_specs=[
            pl.BlockSpec((1, tm, D_MODEL), lambda b, i: (b, i, 0)),
            pl.BlockSpec((1, D_MODEL), const),
            pl.BlockSpec((D_MODEL, D_IN_PROJ), const, pipeline_mode=pl.Buffered(1)),
        ] + [weight_slice(w) for w in weights],
        out_specs=[
            pl.BlockSpec((1, tm, D_QKV), lambda b, i: (b, i, 0)),
            pl.BlockSpec((1, D_CONV // LANES, tm, LANES), lambda b, i: (b, 0, i, 0)),
        ] + [weight_slice(w) for w in weights],
        out_shape=[
            jax.ShapeDtypeStruct((B, S, D_QKV), jnp.float32),
            jax.ShapeDtypeStruct((B, D_CONV // LANES, S, LANES), jnp.float32),
        ] + [jax.ShapeDtypeStruct(w.shape, jnp.bfloat16) for w in weights],
        scratch_shapes=[pltpu.VMEM((D_MODEL, D_IN_PROJ), jnp.bfloat16)],
        compiler_params=pltpu.CompilerParams(
            dimension_semantics=("arbitrary", "arbitrary"),
            vmem_limit_bytes=VMEM_LIMIT_BYTES),
        name="in_proj",
    )(x, g1, w_in, *weights)


def _attn_kernel(q_ref, k_ref, v_ref, bias_rows_ref, out_ref, bias_ref, sub_ref, o_ref, l_ref, m_ref):
    assert HEADS_PER_GROUP == 2
    S = out_ref.shape[1]
    first_head = pl.program_id(1) * HEADS_PER_GROUP

    @pl.when((pl.program_id(0) == 0) & (pl.program_id(1) == 0))
    def _():
        for br in range(len(BRANCHES)):
            for h in range(N_HEADS):
                row = jnp.broadcast_to(bias_rows_ref[br, h:h + 1, :], (WIN, BIAS_ROW))
                table = pltpu.roll(row, BIAS_ROW - WIN, 1, stride=1, stride_axis=0)
                bias_ref[br, h] = table[:, :2 * WIN]
    nat_refs = (q_ref.at[0], k_ref.at[0], v_ref.at[0])
    sub_len = S // SUB_DIL

    for a in range(3):
        for r in range(SUB_DIL):
            sub_ref[a, r * sub_len:(r + 1) * sub_len, :] = nat_refs[a][pl.ds(r, sub_len, stride=SUB_DIL), :]

    def load_rows(a, dil, res, j0, n):
        if dil == 1:
            x = nat_refs[a][pl.ds(j0, n), :]
        else:
            step = dil // SUB_DIL
            start = (res % SUB_DIL) * sub_len + step * j0 + res // SUB_DIL
            x = sub_ref[a, pl.ds(start, n), :] if step == 1 else sub_ref[a, pl.ds(start, n, stride=step), :]
        return x.astype(jnp.bfloat16)

    lane = lax.broadcasted_iota(jnp.int32, (1, LANES), 1)
    head_lanes = [(lane >= h * HEAD_DIM) & (lane < (h + 1) * HEAD_DIM) for h in range(HEADS_PER_GROUP)]
    head_bits = [jnp.where(hl, jnp.uint32(0xFFFFFFFF), jnp.uint32(0)) for hl in head_lanes]
    head_ones = [jnp.where(hl, 1.0, 0.0).astype(jnp.bfloat16) for hl in head_lanes]

    def keep_head(x, h):
        return pltpu.bitcast(pltpu.bitcast(x, jnp.uint32) & head_bits[h], jnp.bfloat16)

    def window_block(br, dil, res, n, dst_start, dst_stride):
        first = n == 0
        nk = WIN if first else 2 * WIN
        k0 = n * WIN if first else (n - 1) * WIN
        q2 = load_rows(0, dil, res, n * WIN, WIN)
        k2 = load_rows(1, dil, res, k0, nk)
        v2 = load_rows(2, dil, res, k0, nk)
        q_both = jnp.concatenate([keep_head(q2, h) for h in range(HEADS_PER_GROUP)], axis=0)
        s = lax.dot_general(q_both, k2, (((1,), (1,)), ((), ())), preferred_element_type=jnp.float32)
        bias = bias_ref[br, pl.ds(first_head, HEADS_PER_GROUP), :, 2 * WIN - nk:2 * WIN]
        s = s + bias.reshape(HEADS_PER_GROUP * WIN, nk)
        m = jnp.max(s, axis=-1, keepdims=True)
        p = jnp.exp2(s - m).astype(jnp.bfloat16)
        p_cat = jnp.concatenate([p[h * WIN:(h + 1) * WIN] for h in range(HEADS_PER_GROUP)], axis=1)
        v_cat = jnp.concatenate(
            [jnp.concatenate([keep_head(v2, h), jnp.broadcast_to(head_ones[h], (nk, LANES))], axis=1)
             for h in range(HEADS_PER_GROUP)], axis=0)
        r = jnp.dot(p_cat, v_cat, preferred_element_type=jnp.float32)
        o_ref[br, rows(dst_start, dst_stride), :] = r[:, :LANES]
        l_ref[br, rows(dst_start, dst_stride), :] = r[:, LANES:]
        m_ref[br, rows(dst_start, dst_stride), :] = jnp.where(head_lanes[0], m[:WIN], m[WIN:])

    def rows(start, stride):
        return pl.ds(start, WIN) if stride == 1 else pl.ds(start, WIN, stride=stride)

    gathered = []
    for br, (_, dil) in enumerate(BRANCHES):
        assert dil == 1 or dil % SUB_DIL == 0
        nb = S // dil // WIN
        gathered.append(dil % 8 == 0)
        assert not gathered[br] or nb == 1
        for first in (True, False):
            for res in range(dil):
                for n in range(0 if first else 1, 1 if first else nb):
                    dst = (res * RESIDUE_PITCH, 1) if gathered[br] else (res + dil * WIN * n, dil)
                    window_block(br, dil, res, n, *dst)

    def merge(c, carry):
        def branch_rows(ref, br):
            if not gathered[br]:
                return ref[br, pl.ds(pl.multiple_of(c * MERGE_ROWS, MERGE_ROWS), MERGE_ROWS), :]
            dil = BRANCHES[br][1]
            j0 = c * (MERGE_ROWS // dil)
            return jnp.concatenate(
                [ref[br, pl.ds(j0 + i, dil, stride=RESIDUE_PITCH), :] for i in range(MERGE_ROWS // dil)],
                axis=0)

        ms = [branch_rows(m_ref, br) for br in range(3)]
        m_all = jnp.maximum(jnp.maximum(ms[0], ms[1]), ms[2])
        ws = [jnp.exp2(mb - m_all) for mb in ms]
        num = sum(ws[br] * branch_rows(o_ref, br) for br in range(3))
        den = sum(ws[br] * branch_rows(l_ref, br) for br in range(3))
        out_rows = pl.ds(pl.multiple_of(c * MERGE_ROWS, MERGE_ROWS), MERGE_ROWS)
        out_ref[0, out_rows, :] = (num / den).astype(out_ref.dtype)
        return carry

    lax.fori_loop(0, S // MERGE_ROWS, merge, 0)


def _attention(qkv, bias_rows):
    B, S, _ = qkv.shape
    n_br = len(BRANCHES)
    qkv_specs = [pl.BlockSpec((1, S, LANES), lambda b, p, part=part: (b, 0, part * N_GROUPS + p))
                 for part in range(3)]
    return pl.pallas_call(
        _attn_kernel,
        grid=(B, N_GROUPS),
        in_specs=qkv_specs + [pl.BlockSpec((n_br, N_HEADS, BIAS_ROW), lambda b, p: (0, 0, 0))],
        out_specs=pl.BlockSpec((1, S, LANES), lambda b, p: (b, 0, p)),
        out_shape=jax.ShapeDtypeStruct((B, S, D_ATTN), jnp.bfloat16),
        scratch_shapes=[pltpu.VMEM((n_br, N_HEADS, WIN, 2 * WIN), jnp.float32),
                        pltpu.VMEM((3, S, LANES), jnp.float32)] +
                       [pltpu.VMEM((n_br, max(S, max(d for _, d in BRANCHES) * RESIDUE_PITCH), LANES),
                                   jnp.float32)] * 3,
        compiler_params=pltpu.CompilerParams(
            dimension_semantics=("arbitrary", "arbitrary"),
            vmem_limit_bytes=VMEM_LIMIT_BYTES),
        name="dilated_attn",
    )(qkv, qkv, qkv, bias_rows)


def _t5_causal_bucket(distance):
    max_exact = NUM_BUCKETS // 2
    d = jnp.maximum(distance, 1).astype(jnp.float32)
    large = max_exact + (jnp.log(d / max_exact) / math.log(MAX_DISTANCE / max_exact)
                         * (NUM_BUCKETS - max_exact)).astype(jnp.int32)
    large = jnp.minimum(large, NUM_BUCKETS - 1)
    return jnp.where(distance < max_exact, distance, large)


def _bias_rows(rel_bias):
    dist = 2 * WIN - jnp.arange(BIAS_ROW, dtype=jnp.int32)
    valid = (dist >= 0) & (dist <= WIN)
    rows = []
    for _, dil in BRANCHES:
        bucket = _t5_causal_bucket(jnp.clip(dist, 0, WIN) * dil)
        bias = jnp.zeros((N_HEADS, BIAS_ROW), jnp.float32)
        for b in range(NUM_BUCKETS):
            bias = jnp.where(bucket[None] == b, rel_bias[b].astype(jnp.float32)[:, None], bias)
        rows.append(jnp.where(valid[None], bias * LOG2E, NEG_INF))
    return jnp.stack(rows)


def _ffn_kernel(x_ref, attn_ref, u_ref, uh_ref, cw_ref, cb_ref, lng_ref, lnb_ref,
                wo_ref, g2_ref, w1_ref, w2_ref, gf_ref, out_ref, ubuf_ref, conv_ref, *, tiles_per_seq):
    tm = FFN_ROWS
    step = pl.program_id(0)
    last_step = pl.num_programs(0) - 1
    n_slabs = D_CONV // LANES
    n_ff = D_FF // FF_CHUNK
    n_chunks = tm // CONV_CHUNK
    off = CONV_HALO - (CONV_WIDTH - 1)

    def exact_zero(tile):
        half_word = jnp.uint32(16)
        bits = lax.shift_right_logical(pltpu.bitcast(tile, jnp.uint32), half_word)
        return pltpu.bitcast(lax.shift_right_logical(bits, half_word), jnp.float32)

    def conv_chunk(c):
        base = c * CONV_CHUNK
        slabs = []
        for s in range(n_slabs):
            cols = slice(s * LANES, (s + 1) * LANES)
            a = jnp.zeros((CONV_CHUNK, LANES), jnp.float32) + cb_ref[:, cols]
            for j in range(CONV_WIDTH):
                tap = ubuf_ref[s, pl.ds(base + off + j, CONV_CHUNK, stride=1), :]
                a = a + tap * cw_ref[j:j + 1, cols]
            slabs.append(a)
        y = jnp.concatenate(slabs, axis=-1)
        mu = jnp.mean(y, axis=-1, keepdims=True)
        cen = y - mu
        var = jnp.mean(cen * cen, axis=-1, keepdims=True)
        y = cen * lax.rsqrt(var + LN_EPS) * lng_ref[...] + lnb_ref[...]
        y = y * jax.nn.sigmoid(y)
        conv_ref[base:base + CONV_CHUNK, :] = y.astype(jnp.bfloat16)
        fold = y[0:8, 0:LANES]
        for r in range(CONV_CHUNK // 8):
            for s in range(n_slabs):
                if r or s:
                    fold = jnp.maximum(fold, y[8 * r:8 * r + 8, s * LANES:(s + 1) * LANES])
        return exact_zero(fold)

    def step_body(ffn_half, conv_half):
        if ffn_half:
            mixed = jnp.concatenate([attn_ref[0], conv_ref[...]], axis=-1)
            x1 = x_ref[0] + jnp.dot(mixed, wo_ref[...], preferred_element_type=jnp.float32)
            inv = lax.rsqrt(jnp.mean(x1 * x1, axis=-1, keepdims=True) + RMS_EPS)
            h = (x1 * inv * g2_ref[...]).astype(jnp.bfloat16)
        if conv_half:
            seq_start = (step % tiles_per_seq) == 0
            ubuf_ref[:, 0:CONV_HALO, :] = jnp.where(seq_start, 0.0, uh_ref[0])
            ubuf_ref[:, CONV_HALO:CONV_HALO + tm, :] = u_ref[0]
        if not ffn_half:
            for c in range(n_chunks):
                conv_chunk(c)
            return

        chunk_ids = iter(range(n_chunks))
        anchor_cols = FF_CHUNK * n_ff // n_chunks
        assert anchor_cols % LANES == 0

        def anchored(d):
            top = []
            for g in range(d.shape[1] // anchor_cols):
                top.append(d[0:8, anchor_cols * g:anchor_cols * g + LANES] + conv_chunk(next(chunk_ids)))
                if anchor_cols > LANES:
                    top.append(d[0:8, anchor_cols * g + LANES:anchor_cols * (g + 1)])
            return jnp.concatenate([jnp.concatenate(top, axis=1), d[8:]], axis=0)

        acc = x1
        for c in range(n_ff):
            cols = slice(c * FF_CHUNK, (c + 1) * FF_CHUNK)
            d = jnp.dot(h, w1_ref[:, cols], preferred_element_type=jnp.float32)
            f = jnp.maximum(anchored(d) if conv_half else d, 0.0)
            f = (f * f).astype(jnp.bfloat16)
            acc = acc + jnp.dot(f, w2_ref[cols, :], preferred_element_type=jnp.float32)
        inv = lax.rsqrt(jnp.mean(acc * acc, axis=-1, keepdims=True) + RMS_EPS)
        out_ref[0] = acc * inv * gf_ref[...]

    pl.when(step == 0)(lambda: step_body(False, True))
    pl.when((step > 0) & (step < last_step))(lambda: step_body(True, True))
    pl.when(step == last_step)(lambda: step_body(True, False))


def _ffn(x, attn, u, conv_w, conv_b, ln_g, ln_b, w_o, g2, w1, w2, gf):
    B, S, _ = x.shape
    tm = FFN_ROWS
    tiles_per_seq = S // tm
    n_tiles = B * tiles_per_seq
    halo_blocks = tm // CONV_HALO
    x = x.reshape(n_tiles, tm, D_MODEL)
    attn = attn.reshape(n_tiles, tm, D_ATTN)

    def ffn_tile(s):
        return (jnp.maximum(s - 1, 0), 0, 0)

    def conv_tile(s):
        t = jnp.minimum(s, n_tiles - 1)
        return t // tiles_per_seq, t % tiles_per_seq

    def u_block(s):
        b, i = conv_tile(s)
        return (b, 0, i, 0)

    def halo_block(s):
        b, i = conv_tile(s)
        return (b, 0, jnp.maximum(i * halo_blocks - 1, 0), 0)

    const = lambda s: (0, 0)
    resident = functools.partial(pl.BlockSpec, index_map=const, pipeline_mode=pl.Buffered(1))
    out = pl.pallas_call(
        functools.partial(_ffn_kernel, tiles_per_seq=tiles_per_seq),
        grid=(n_tiles + 1,),
        in_specs=[
            pl.BlockSpec((1, tm, D_MODEL), ffn_tile),
            pl.BlockSpec((1, tm, D_ATTN), ffn_tile),
            pl.BlockSpec((1, D_CONV // LANES, tm, LANES), u_block),
            pl.BlockSpec((1, D_CONV // LANES, CONV_HALO, LANES), halo_block),
            pl.BlockSpec((CONV_WIDTH, D_CONV), const),
            pl.BlockSpec((1, D_CONV), const),
            pl.BlockSpec((1, D_CONV), const),
            pl.BlockSpec((1, D_CONV), const),
            resident((D_MODEL, D_MODEL)),
            pl.BlockSpec((1, D_MODEL), const),
            resident((D_MODEL, D_FF)),
            resident((D_FF, D_MODEL)),
            pl.BlockSpec((1, D_MODEL), const),
        ],
        out_specs=pl.BlockSpec((1, tm, D_MODEL), ffn_tile),
        out_shape=jax.ShapeDtypeStruct((n_tiles, tm, D_MODEL), jnp.float32),
        scratch_shapes=[
            pltpu.VMEM((D_CONV // LANES, CONV_HALO + tm, LANES), jnp.float32),
            pltpu.VMEM((tm, D_CONV), jnp.bfloat16),
        ],
        compiler_params=pltpu.CompilerParams(
            dimension_semantics=("arbitrary",),
            vmem_limit_bytes=VMEM_LIMIT_BYTES),
        name="conv_outproj_ffn",
    )(x, attn, u, u, conv_w, conv_b, ln_g, ln_b, w_o, g2, w1, w2, gf)
    return out.reshape(B, S, D_MODEL)


def kernel(x, norm1_g, w_in, conv_w, conv_b, conv_ln_g, conv_ln_b, w_o, norm2_g, w_ff1, w_ff2,
           rel_bias, final_g):
    assert norm1_g.shape[0] == 1, "single-layer problem"
    qkv, u, w_o_bf, w1_bf, w2_bf = _project(x, norm1_g, w_in[0], w_o[0], w_ff1[0], w_ff2[0])
    attn = _attention(qkv, _bias_rows(rel_bias))
    return _ffn(x, attn, u, conv_w[0], conv_b, conv_ln_g, conv_ln_b, w_o_bf, norm2_g,
                w1_bf, w2_bf, final_g.reshape(1, D_MODEL))
```

```python
import functools
import math

import jax
import jax.numpy as jnp
from jax import lax
from jax.experimental import pallas as pl
from jax.experimental.pallas import tpu as pltpu

D_MODEL = 1024
N_HEADS = 8
HEAD_DIM = 64
D_ATTN = N_HEADS * HEAD_DIM
D_CONV = D_MODEL - D_ATTN
D_QKV = 3 * D_ATTN
D_IN_PROJ = D_QKV + 2 * D_CONV
CONV_WIDTH = 31
D_FF = 4 * D_MODEL
BRANCHES = ((128, 1), (512, 4), (2048, 16))
WIN = 128
NUM_BUCKETS = 32
MAX_DISTANCE = 2048
RMS_EPS = 1e-6
LN_EPS = 1e-5
NEG_INF = -1e30
LOG2E = math.log2(math.e)

LANES = 128
BF16_TILE_ROWS = 16
PROJ_COLS = 256
HEADS_PER_GROUP = LANES // HEAD_DIM
N_GROUPS = N_HEADS // HEADS_PER_GROUP
VMEM_LIMIT_BYTES = 56 * 1024 * 1024

PROJ_ROWS = 1024
CONV_HALO = 32
CONV_CHUNK = 32
SUB_DIL = 4
RESIDUE_PITCH = 136
MERGE_ROWS = 512
BIAS_ROW = 4 * WIN
FFN_ROWS = 512
FF_CHUNK = 512


def _proj_kernel(x_ref, g1_ref, win_f32_ref, qkv_ref, u_ref, win_ref):
    @pl.when((pl.program_id(0) == 0) & (pl.program_id(1) == 0))
    def _():
        win_ref[...] = win_f32_ref[...].astype(jnp.bfloat16)

    x = x_ref[0]
    inv = lax.rsqrt(jnp.mean(x * x, axis=-1, keepdims=True) + RMS_EPS)
    h = (x * inv * g1_ref[...]).astype(jnp.bfloat16)

    slabs_per_group = PROJ_COLS // LANES
    for g in range(D_CONV // PROJ_COLS):
        a_cols = slice(D_QKV + g * PROJ_COLS, D_QKV + (g + 1) * PROJ_COLS)
        gate_cols = slice(D_QKV + D_CONV + g * PROJ_COLS, D_QKV + D_CONV + (g + 1) * PROJ_COLS)
        a = jnp.dot(h, win_ref[:, a_cols], preferred_element_type=jnp.float32)
        gate = jnp.dot(h, win_ref[:, gate_cols], preferred_element_type=jnp.float32)
        u = a * jax.nn.sigmoid(gate)
        for k in range(slabs_per_group):
            u_ref[0, slabs_per_group * g + k] = u[:, k * LANES:(k + 1) * LANES]

    q_scale = LOG2E / math.sqrt(HEAD_DIM)
    for g in range(D_QKV // PROJ_COLS):
        cols = slice(g * PROJ_COLS, (g + 1) * PROJ_COLS)
        z = jnp.dot(h, win_ref[:, cols], preferred_element_type=jnp.float32)
        if g * PROJ_COLS < D_ATTN:
            z = z * q_scale
        qkv_ref[0, :, cols] = z


def _project(x, g1, w_in):
    B, S, _ = x.shape
    tm = PROJ_ROWS
    const = lambda b, i: (0, 0)
    return pl.pallas_call(
        _proj_kernel,
        grid=(B, S // tm),
        in_specs=[
            pl.BlockSpec((1, tm, D_MODEL), lambda b, i: (b, i, 0)),
            pl.BlockSpec((1, D_MODEL), const),
            pl.BlockSpec((D_MODEL, D_IN_PROJ), const, pipeline_mode=pl.Buffered(1)),
        ],
        out_specs=[
            pl.BlockSpec((1, tm, D_QKV), lambda b, i: (b, i, 0)),
            pl.BlockSpec((1, D_CONV // LANES, tm, LANES), lambda b, i: (b, 0, i, 0)),
        ],
        out_shape=[
            jax.ShapeDtypeStruct((B, S, D_QKV), jnp.float32),
            jax.ShapeDtypeStruct((B, D_CONV // LANES, S, LANES), jnp.float32),
        ],
        scratch_shapes=[pltpu.VMEM((D_MODEL, D_IN_PROJ), jnp.bfloat16)],
        compiler_params=pltpu.CompilerParams(
            dimension_semantics=("arbitrary", "arbitrary"),
            vmem_limit_bytes=VMEM_LIMIT_BYTES),
        name="in_proj",
    )(x, g1, w_in)


def _attn_kernel(q_ref, k_ref, v_ref, bias_rows_ref, wo_ref, w1_ref, w2_ref,
                 out_ref, wo_bf_ref, w1_bf_ref, w2_bf_ref, bias_ref, sub_ref, o_ref, l_ref, m_ref):
    assert HEADS_PER_GROUP == 2
    S = out_ref.shape[1]

    wo_bf_ref[...] = wo_ref[...].astype(jnp.bfloat16)
    w1_bf_ref[...] = w1_ref[...].astype(jnp.bfloat16)
    w2_bf_ref[...] = w2_ref[...].astype(jnp.bfloat16)
    first_head = pl.program_id(1) * HEADS_PER_GROUP

    @pl.when((pl.program_id(0) == 0) & (pl.program_id(1) == 0))
    def _():
        for br in range(len(BRANCHES)):
            for h in range(N_HEADS):
                row = jnp.broadcast_to(bias_rows_ref[br, h:h + 1, :], (WIN, BIAS_ROW))
                table = pltpu.roll(row, BIAS_ROW - WIN, 1, stride=1, stride_axis=0)
                bias_ref[br, h] = table[:, :2 * WIN]
    nat_refs = (q_ref.at[0], k_ref.at[0], v_ref.at[0])
    sub_len = S // SUB_DIL

    for a in range(3):
        for r in range(SUB_DIL):
            sub_ref[a, r * sub_len:(r + 1) * sub_len, :] = nat_refs[a][pl.ds(r, sub_len, stride=SUB_DIL), :]

    def load_rows(a, dil, res, j0, n):
        if dil == 1:
            x = nat_refs[a][pl.ds(j0, n), :]
        else:
            step = dil // SUB_DIL
            start = (res % SUB_DIL) * sub_len + step * j0 + res // SUB_DIL
            x = sub_ref[a, pl.ds(start, n), :] if step == 1 else sub_ref[a, pl.ds(start, n, stride=step), :]
        return x.astype(jnp.bfloat16)

    lane = lax.broadcasted_iota(jnp.int32, (1, LANES), 1)
    head_lanes = [(lane >= h * HEAD_DIM) & (lane < (h + 1) * HEAD_DIM) for h in range(HEADS_PER_GROUP)]
    head_bits = [jnp.where(hl, jnp.uint32(0xFFFFFFFF), jnp.uint32(0)) for hl in head_lanes]
    head_ones = [jnp.where(hl, 1.0, 0.0).astype(jnp.bfloat16) for hl in head_lanes]

    def keep_head(x, h):
        return pltpu.bitcast(pltpu.bitcast(x, jnp.uint32) & head_bits[h], jnp.bfloat16)

    def window_block(br, dil, res, n, dst_start, dst_stride):
        first = n == 0
        nk = WIN if first else 2 * WIN
        k0 = n * WIN if first else (n - 1) * WIN
        q2 = load_rows(0, dil, res, n * WIN, WIN)
        k2 = load_rows(1, dil, res, k0, nk)
        v2 = load_rows(2, dil, res, k0, nk)
        q_both = jnp.concatenate([keep_head(q2, h) for h in range(HEADS_PER_GROUP)], axis=0)
        s = lax.dot_general(q_both, k2, (((1,), (1,)), ((), ())), preferred_element_type=jnp.float32)
        bias = bias_ref[br, pl.ds(first_head, HEADS_PER_GROUP), :, 2 * WIN - nk:2 * WIN]
        s = s + bias.reshape(HEADS_PER_GROUP * WIN, nk)
        m = jnp.max(s, axis=-1, keepdims=True)
        p = jnp.exp2(s - m).astype(jnp.bfloat16)
        p_cat = jnp.concatenate([p[h * WIN:(h + 1) * WIN] for h in range(HEADS_PER_GROUP)], axis=1)
        v_cat = jnp.concatenate(
            [jnp.concatenate([keep_head(v2, h), jnp.broadcast_to(head_ones[h], (nk, LANES))], axis=1)
             for h in range(HEADS_PER_GROUP)], axis=0)
        r = jnp.dot(p_cat, v_cat, preferred_element_type=jnp.float32)
        o_ref[br, rows(dst_start, dst_stride), :] = r[:, :LANES]
        l_ref[br, rows(dst_start, dst_stride), :] = r[:, LANES:]
        m_ref[br, rows(dst_start, dst_stride), :] = jnp.where(head_lanes[0], m[:WIN], m[WIN:])

    def rows(start, stride):
        return pl.ds(start, WIN) if stride == 1 else pl.ds(start, WIN, stride=stride)

    gathered = []
    for br, (_, dil) in enumerate(BRANCHES):
        assert dil == 1 or dil % SUB_DIL == 0
        nb = S // dil // WIN
        gathered.append(dil % 8 == 0)
        assert not gathered[br] or nb == 1
        for first in (True, False):
            for res in range(dil):
                for n in range(0 if first else 1, 1 if first else nb):
                    dst = (res * RESIDUE_PITCH, 1) if gathered[br] else (res + dil * WIN * n, dil)
                    window_block(br, dil, res, n, *dst)

    def merge(c, carry):
        def branch_rows(ref, br):
            if not gathered[br]:
                return ref[br, pl.ds(pl.multiple_of(c * MERGE_ROWS, MERGE_ROWS), MERGE_ROWS), :]
            dil = BRANCHES[br][1]
            j0 = c * (MERGE_ROWS // dil)
            return jnp.concatenate(
                [ref[br, pl.ds(j0 + i, dil, stride=RESIDUE_PITCH), :] for i in range(MERGE_ROWS // dil)],
                axis=0)

        ms = [branch_rows(m_ref, br) for br in range(3)]
        m_all = jnp.maximum(jnp.maximum(ms[0], ms[1]), ms[2])
        ws = [jnp.exp2(mb - m_all) for mb in ms]
        num = sum(ws[br] * branch_rows(o_ref, br) for br in range(3))
        den = sum(ws[br] * branch_rows(l_ref, br) for br in range(3))
        out_rows = pl.ds(pl.multiple_of(c * MERGE_ROWS, MERGE_ROWS), MERGE_ROWS)
        out_ref[0, out_rows, :] = (num / den).astype(out_ref.dtype)
        return carry

    lax.fori_loop(0, S // MERGE_ROWS, merge, 0)


def _attention(qkv, bias_rows, weights):
    B, S, _ = qkv.shape
    n_br = len(BRANCHES)
    n_steps = B * N_GROUPS
    qkv_specs = [pl.BlockSpec((1, S, LANES), lambda b, p, part=part: (b, 0, part * N_GROUPS + p))
                 for part in range(3)]

    def weight_slice(w):
        rows = w.shape[0] // n_steps
        assert rows * n_steps == w.shape[0] and rows % BF16_TILE_ROWS == 0
        return pl.BlockSpec((rows, w.shape[1]), lambda b, p: (b * N_GROUPS + p, 0))

    return pl.pallas_call(
        _attn_kernel,
        grid=(B, N_GROUPS),
        in_specs=qkv_specs + [pl.BlockSpec((n_br, N_HEADS, BIAS_ROW), lambda b, p: (0, 0, 0))] +
                 [weight_slice(w) for w in weights],
        out_specs=[pl.BlockSpec((1, S, LANES), lambda b, p: (b, 0, p))] +
                  [weight_slice(w) for w in weights],
        out_shape=[jax.ShapeDtypeStruct((B, S, D_ATTN), jnp.bfloat16)] +
                  [jax.ShapeDtypeStruct(w.shape, jnp.bfloat16) for w in weights],
        scratch_shapes=[pltpu.VMEM((n_br, N_HEADS, WIN, 2 * WIN), jnp.float32),
                        pltpu.VMEM((3, S, LANES), jnp.float32)] +
                       [pltpu.VMEM((n_br, max(S, max(d for _, d in BRANCHES) * RESIDUE_PITCH), LANES),
                                   jnp.float32)] * 3,
        compiler_params=pltpu.CompilerParams(
            dimension_semantics=("arbitrary", "arbitrary"),
            vmem_limit_bytes=VMEM_LIMIT_BYTES),
        name="dilated_attn",
    )(qkv, qkv, qkv, bias_rows, *weights)


def _t5_causal_bucket(distance):
    max_exact = NUM_BUCKETS // 2
    d = jnp.maximum(distance, 1).astype(jnp.float32)
    large = max_exact + (jnp.log(d / max_exact) / math.log(MAX_DISTANCE / max_exact)
                         * (NUM_BUCKETS - max_exact)).astype(jnp.int32)
    large = jnp.minimum(large, NUM_BUCKETS - 1)
    return jnp.where(distance < max_exact, distance, large)


def _bias_rows(rel_bias):
    dist = 2 * WIN - jnp.arange(BIAS_ROW, dtype=jnp.int32)
    valid = (dist >= 0) & (dist <= WIN)
    rows = []
    for _, dil in BRANCHES:
        bucket = _t5_causal_bucket(jnp.clip(dist, 0, WIN) * dil)
        bias = jnp.zeros((N_HEADS, BIAS_ROW), jnp.float32)
        for b in range(NUM_BUCKETS):
            bias = jnp.where(bucket[None] == b, rel_bias[b].astype(jnp.float32)[:, None], bias)
        rows.append(jnp.where(valid[None], bias * LOG2E, NEG_INF))
    return jnp.stack(rows)


def _ffn_kernel(x_ref, attn_ref, u_ref, uh_ref, cw_ref, cb_ref, lng_ref, lnb_ref,
                wo_ref, g2_ref, w1_ref, w2_ref, gf_ref, out_ref, ubuf_ref, conv_ref, *, tiles_per_seq):
    tm = FFN_ROWS
    step = pl.program_id(0)
    last_step = pl.num_programs(0) - 1
    n_slabs = D_CONV // LANES
    n_ff = D_FF // FF_CHUNK
    n_chunks = tm // CONV_CHUNK
    off = CONV_HALO - (CONV_WIDTH - 1)

    def exact_zero(tile):
        half_word = jnp.uint32(16)
        bits = lax.shift_right_logical(pltpu.bitcast(tile, jnp.uint32), half_word)
        return pltpu.bitcast(lax.shift_right_logical(bits, half_word), jnp.float32)

    def conv_chunk(c):
        base = c * CONV_CHUNK
        slabs = []
        for s in range(n_slabs):
            cols = slice(s * LANES, (s + 1) * LANES)
            a = jnp.zeros((CONV_CHUNK, LANES), jnp.float32) + cb_ref[:, cols]
            for j in range(CONV_WIDTH):
                tap = ubuf_ref[s, pl.ds(base + off + j, CONV_CHUNK, stride=1), :]
                a = a + tap * cw_ref[j:j + 1, cols]
            slabs.append(a)
        y = jnp.concatenate(slabs, axis=-1)
        mu = jnp.mean(y, axis=-1, keepdims=True)
        cen = y - mu
        var = jnp.mean(cen * cen, axis=-1, keepdims=True)
        y = cen * lax.rsqrt(var + LN_EPS) * lng_ref[...] + lnb_ref[...]
        y = y * jax.nn.sigmoid(y)
        conv_ref[base:base + CONV_CHUNK, :] = y.astype(jnp.bfloat16)
        fold = y[0:8, 0:LANES]
        for r in range(CONV_CHUNK // 8):
            for s in range(n_slabs):
                if r or s:
                    fold = jnp.maximum(fold, y[8 * r:8 * r + 8, s * LANES:(s + 1) * LANES])
        return exact_zero(fold)

    def step_body(ffn_half, conv_half):
        if ffn_half:
            mixed = jnp.concatenate([attn_ref[0], conv_ref[...]], axis=-1)
            x1 = x_ref[0] + jnp.dot(mixed, wo_ref[...], preferred_element_type=jnp.float32)
            inv = lax.rsqrt(jnp.mean(x1 * x1, axis=-1, keepdims=True) + RMS_EPS)
            h = (x1 * inv * g2_ref[...]).astype(jnp.bfloat16)
        if conv_half:
            seq_start = (step % tiles_per_seq) == 0
            ubuf_ref[:, 0:CONV_HALO, :] = jnp.where(seq_start, 0.0, uh_ref[0])
            ubuf_ref[:, CONV_HALO:CONV_HALO + tm, :] = u_ref[0]
        if not ffn_half:
            for c in range(n_chunks):
                conv_chunk(c)
            return

        chunk_ids = iter(range(n_chunks))
        anchor_cols = FF_CHUNK * n_ff // n_chunks
        assert anchor_cols % LANES == 0

        def anchored(d):
            top = []
            for g in range(d.shape[1] // anchor_cols):
                top.append(d[0:8, anchor_cols * g:anchor_cols * g + LANES] + conv_chunk(next(chunk_ids)))
                if anchor_cols > LANES:
                    top.append(d[0:8, anchor_cols * g + LANES:anchor_cols * (g + 1)])
            return jnp.concatenate([jnp.concatenate(top, axis=1), d[8:]], axis=0)

        acc = x1
        for c in range(n_ff):
            cols = slice(c * FF_CHUNK, (c + 1) * FF_CHUNK)
            d = jnp.dot(h, w1_ref[:, cols], preferred_element_type=jnp.float32)
            f = jnp.maximum(anchored(d) if conv_half else d, 0.0)
            f = (f * f).astype(jnp.bfloat16)
            acc = acc + jnp.dot(f, w2_ref[cols, :], preferred_element_type=jnp.float32)
        inv = lax.rsqrt(jnp.mean(acc * acc, axis=-1, keepdims=True) + RMS_EPS)
        out_ref[0] = acc * inv * gf_ref[...]

    pl.when(step == 0)(lambda: step_body(False, True))
    pl.when((step > 0) & (step < last_step))(lambda: step_body(True, True))
    pl.when(step == last_step)(lambda: step_body(True, False))


def _ffn(x, attn, u, conv_w, conv_b, ln_g, ln_b, w_o, g2, w1, w2, gf):
    B, S, _ = x.shape
    tm = FFN_ROWS
    tiles_per_seq = S // tm
    n_tiles = B * tiles_per_seq
    halo_blocks = tm // CONV_HALO
    x = x.reshape(n_tiles, tm, D_MODEL)
    attn = attn.reshape(n_tiles, tm, D_ATTN)

    def ffn_tile(s):
        return (jnp.maximum(s - 1, 0), 0, 0)

    def conv_tile(s):
        t = jnp.minimum(s, n_tiles - 1)
        return t // tiles_per_seq, t % tiles_per_seq

    def u_block(s):
        b, i = conv_tile(s)
        return (b, 0, i, 0)

    def halo_block(s):
        b, i = conv_tile(s)
        return (b, 0, jnp.maximum(i * halo_blocks - 1, 0), 0)

    const = lambda s: (0, 0)
    resident = functools.partial(pl.BlockSpec, index_map=const, pipeline_mode=pl.Buffered(1))
    out = pl.pallas_call(
        functools.partial(_ffn_kernel, tiles_per_seq=tiles_per_seq),
        grid=(n_tiles + 1,),
        in_specs=[
            pl.BlockSpec((1, tm, D_MODEL), ffn_tile),
            pl.BlockSpec((1, tm, D_ATTN), ffn_tile),
            pl.BlockSpec((1, D_CONV // LANES, tm, LANES), u_block),
            pl.BlockSpec((1, D_CONV // LANES, CONV_HALO, LANES), halo_block),
            pl.BlockSpec((CONV_WIDTH, D_CONV), const),
            pl.BlockSpec((1, D_CONV), const),
            pl.BlockSpec((1, D_CONV), const),
            pl.BlockSpec((1, D_CONV), const),
            resident((D_MODEL, D_MODEL)),
            pl.BlockSpec((1, D_MODEL), const),
            resident((D_MODEL, D_FF)),
            resident((D_FF, D_MODEL)),
            pl.BlockSpec((1, D_MODEL), const),
        ],
        out_specs=pl.BlockSpec((1, tm, D_MODEL), ffn_tile),
        out_shape=jax.ShapeDtypeStruct((n_tiles, tm, D_MODEL), jnp.float32),
        scratch_shapes=[
            pltpu.VMEM((D_CONV // LANES, CONV_HALO + tm, LANES), jnp.float32),
            pltpu.VMEM((tm, D_CONV), jnp.bfloat16),
        ],
        compiler_params=pltpu.CompilerParams(
            dimension_semantics=("arbitrary",),
            vmem_limit_bytes=VMEM_LIMIT_BYTES),
        name="conv_outproj_ffn",
    )(x, attn, u, u, conv_w, conv_b, ln_g, ln_b, w_o, g2, w1, w2, gf)
    return out.reshape(B, S, D_MODEL)


def kernel(x, norm1_g, w_in, conv_w, conv_b, conv_ln_g, conv_ln_b, w_o, norm2_g, w_ff1, w_ff2,
           rel_bias, final_g):
    assert norm1_g.shape[0] == 1, "single-layer problem"
    qkv, u = _project(x, norm1_g, w_in[0])
    attn, w_o_bf, w1_bf, w2_bf = _attention(qkv, _bias_rows(rel_bias), (w_o[0], w_ff1[0], w_ff2[0]))
    return _ffn(x, attn, u, conv_w[0], conv_b, conv_ln_g, conv_ln_b, w_o_bf, norm2_g,
                w1_bf, w2_bf, final_g.reshape(1, D_MODEL))
```

```python
import functools
import math

import jax
import jax.numpy as jnp
from jax import lax
from jax.experimental import pallas as pl
from jax.experimental.pallas import tpu as pltpu

D_MODEL = 1024
N_HEADS = 8
HEAD_DIM = 64
D_ATTN = N_HEADS * HEAD_DIM
D_CONV = D_MODEL - D_ATTN
D_QKV = 3 * D_ATTN
D_IN_PROJ = D_QKV + 2 * D_CONV
CONV_WIDTH = 31
D_FF = 4 * D_MODEL
BRANCHES = ((128, 1), (512, 4), (2048, 16))
WIN = 128
NUM_BUCKETS = 32
MAX_DISTANCE = 2048
RMS_EPS = 1e-6
LN_EPS = 1e-5
NEG_INF = -1e30
LOG2E = math.log2(math.e)

LANES = 128
BF16_TILE_ROWS = 16
PROJ_COLS = 256
HEADS_PER_GROUP = LANES // HEAD_DIM
N_GROUPS = N_HEADS // HEADS_PER_GROUP
VMEM_LIMIT_BYTES = 56 * 1024 * 1024

PROJ_ROWS = 1024
CONV_HALO = 32
CONV_CHUNK = 32
SUB_DIL = 4
RESIDUE_PITCH = 136
ATTN_UNITS = 2
MERGE_ROWS = 512
BIAS_ROW = 4 * WIN
FFN_ROWS = 512
FF_CHUNK = 512


def _proj_kernel(x_ref, g1_ref, win_f32_ref, wo_ref, w1_ref, w2_ref,
                 qkv_ref, u_ref, wo_bf_ref, w1_bf_ref, w2_bf_ref, win_ref):
    @pl.when((pl.program_id(0) == 0) & (pl.program_id(1) == 0))
    def _():
        win_ref[...] = win_f32_ref[...].astype(jnp.bfloat16)

    wo_bf_ref[...] = wo_ref[...].astype(jnp.bfloat16)
    w1_bf_ref[...] = w1_ref[...].astype(jnp.bfloat16)
    w2_bf_ref[...] = w2_ref[...].astype(jnp.bfloat16)

    x = x_ref[0]
    inv = lax.rsqrt(jnp.mean(x * x, axis=-1, keepdims=True) + RMS_EPS)
    h = (x * inv * g1_ref[...]).astype(jnp.bfloat16)

    slabs_per_group = PROJ_COLS // LANES
    for g in range(D_CONV // PROJ_COLS):
        a_cols = slice(D_QKV + g * PROJ_COLS, D_QKV + (g + 1) * PROJ_COLS)
        gate_cols = slice(D_QKV + D_CONV + g * PROJ_COLS, D_QKV + D_CONV + (g + 1) * PROJ_COLS)
        a = jnp.dot(h, win_ref[:, a_cols], preferred_element_type=jnp.float32)
        gate = jnp.dot(h, win_ref[:, gate_cols], preferred_element_type=jnp.float32)
        u = a * jax.nn.sigmoid(gate)
        for k in range(slabs_per_group):
            u_ref[0, slabs_per_group * g + k] = u[:, k * LANES:(k + 1) * LANES]

    q_scale = LOG2E / math.sqrt(HEAD_DIM)
    for g in range(D_QKV // PROJ_COLS):
        cols = slice(g * PROJ_COLS, (g + 1) * PROJ_COLS)
        z = jnp.dot(h, win_ref[:, cols], preferred_element_type=jnp.float32)
        if g * PROJ_COLS < D_ATTN:
            z = z * q_scale
        qkv_ref[0, :, cols] = z


def _project(x, g1, w_in, w_o, w1, w2):
    B, S, _ = x.shape
    tm = PROJ_ROWS
    tiles_per_seq = S // tm
    n_steps = B * tiles_per_seq
    const = lambda b, i: (0, 0)
    step_rows = lambda b, i: (b * tiles_per_seq + i, 0)

    def weight_slice(w):
        rows = w.shape[0] // n_steps
        assert rows * n_steps == w.shape[0] and rows % BF16_TILE_ROWS == 0
        return pl.BlockSpec((rows, w.shape[1]), step_rows)

    weights = (w_o, w1, w2)
    return pl.pallas_call(
        _proj_kernel,
        grid=(B, tiles_per_seq),
        in_specs=[
            pl.BlockSpec((1, tm, D_MODEL), lambda b, i: (b, i, 0)),
            pl.BlockSpec((1, D_MODEL), const),
            pl.BlockSpec((D_MODEL, D_IN_PROJ), const, pipeline_mode=pl.Buffered(1)),
        ] + [weight_slice(w) for w in weights],
        out_specs=[
            pl.BlockSpec((1, tm, D_QKV), lambda b, i: (b, i, 0)),
            pl.BlockSpec((1, D_CONV // LANES, tm, LANES), lambda b, i: (b, 0, i, 0)),
        ] + [weight_slice(w) for w in weights],
        out_shape=[
            jax.ShapeDtypeStruct((B, S, D_QKV), jnp.float32),
            jax.ShapeDtypeStruct((B, D_CONV // LANES, S, LANES), jnp.float32),
        ] + [jax.ShapeDtypeStruct(w.shape, jnp.bfloat16) for w in weights],
        scratch_shapes=[pltpu.VMEM((D_MODEL, D_IN_PROJ), jnp.bfloat16)],
        compiler_params=pltpu.CompilerParams(
            dimension_semantics=("arbitrary", "arbitrary"),
            vmem_limit_bytes=VMEM_LIMIT_BYTES),
        name="in_proj",
    )(x, g1, w_in, *weights)


def _attn_kernel(*refs):
    assert HEADS_PER_GROUP == 2
    qkv_refs = refs[:3 * ATTN_UNITS]
    bias_rows_ref, out_ref, bias_ref, sub_all_ref, o_all_ref, l_all_ref, m_all_ref = refs[3 * ATTN_UNITS:]
    S = out_ref.shape[1]

    @pl.when((pl.program_id(0) == 0) & (pl.program_id(1) == 0))
    def _():
        for br in range(len(BRANCHES)):
            for h in range(N_HEADS):
                row = jnp.broadcast_to(bias_rows_ref[br, h:h + 1, :], (WIN, BIAS_ROW))
                table = pltpu.roll(row, BIAS_ROW - WIN, 1, stride=1, stride_axis=0)
                bias_ref[br, h] = table[:, :2 * WIN]

    merges = [_attn_unit_blocks(qkv_refs[3 * g:3 * g + 3], bias_ref,
                                (pl.program_id(1) * ATTN_UNITS + g) * HEADS_PER_GROUP, S,
                                sub_all_ref.at[g], o_all_ref.at[g], l_all_ref.at[g], m_all_ref.at[g])
              for g in range(ATTN_UNITS)]
    for g, merge in enumerate(merges):
        merge(out_ref.at[0, :, g * LANES:(g + 1) * LANES])


def _attn_unit_blocks(qkv_refs, bias_ref, first_head, S, sub_ref, o_ref, l_ref, m_ref):
    nat_refs = tuple(r.at[0] for r in qkv_refs)
    sub_len = S // SUB_DIL

    for a in range(3):
        for r in range(SUB_DIL):
            sub_ref[a, r * sub_len:(r + 1) * sub_len, :] = nat_refs[a][pl.ds(r, sub_len, stride=SUB_DIL), :]

    def load_rows(a, dil, res, j0, n):
        if dil == 1:
            x = nat_refs[a][pl.ds(j0, n), :]
        else:
            step = dil // SUB_DIL
            start = (res % SUB_DIL) * sub_len + step * j0 + res // SUB_DIL
            x = sub_ref[a, pl.ds(start, n), :] if step == 1 else sub_ref[a, pl.ds(start, n, stride=step), :]
        return x.astype(jnp.bfloat16)

    lane = lax.broadcasted_iota(jnp.int32, (1, LANES), 1)
    head_lanes = [(lane >= h * HEAD_DIM) & (lane < (h + 1) * HEAD_DIM) for h in range(HEADS_PER_GROUP)]
    head_bits = [jnp.where(hl, jnp.uint32(0xFFFFFFFF), jnp.uint32(0)) for hl in head_lanes]
    head_ones = [jnp.where(hl, 1.0, 0.0).astype(jnp.bfloat16) for hl in head_lanes]

    def keep_head(x, h):
        return pltpu.bitcast(pltpu.bitcast(x, jnp.uint32) & head_bits[h], jnp.bfloat16)

    def window_block(br, dil, res, n, dst_start, dst_stride):
        first = n == 0
        nk = WIN if first else 2 * WIN
        k0 = n * WIN if first else (n - 1) * WIN
        q2 = load_rows(0, dil, res, n * WIN, WIN)
        k2 = load_rows(1, dil, res, k0, nk)
        v2 = load_rows(2, dil, res, k0, nk)
        q_both = jnp.concatenate([keep_head(q2, h) for h in range(HEADS_PER_GROUP)], axis=0)
        s = lax.dot_general(q_both, k2, (((1,), (1,)), ((), ())), preferred_element_type=jnp.float32)
        bias = bias_ref[br, pl.ds(first_head, HEADS_PER_GROUP), :, 2 * WIN - nk:2 * WIN]
        s = s + bias.reshape(HEADS_PER_GROUP * WIN, nk)
        m = jnp.max(s, axis=-1, keepdims=True)
        p = jnp.exp2(s - m).astype(jnp.bfloat16)
        p_cat = jnp.concatenate([p[h * WIN:(h + 1) * WIN] for h in range(HEADS_PER_GROUP)], axis=1)
        v_cat = jnp.concatenate(
            [jnp.concatenate([keep_head(v2, h), jnp.broadcast_to(head_ones[h], (nk, LANES))], axis=1)
             for h in range(HEADS_PER_GROUP)], axis=0)
        r = jnp.dot(p_cat, v_cat, preferred_element_type=jnp.float32)
        o_ref[br, rows(dst_start, dst_stride), :] = r[:, :LANES]
        l_ref[br, rows(dst_start, dst_stride), :] = r[:, LANES:]
        m_ref[br, rows(dst_start, dst_stride), :] = jnp.where(head_lanes[0], m[:WIN], m[WIN:])

    def rows(start, stride):
        return pl.ds(start, WIN) if stride == 1 else pl.ds(start, WIN, stride=stride)

    gathered = []
    for br, (_, dil) in enumerate(BRANCHES):
        assert dil == 1 or dil % SUB_DIL == 0
        nb = S // dil // WIN
        gathered.append(dil % 8 == 0)
        assert not gathered[br] or nb == 1
        for first in (True, False):
            for res in range(dil):
                for n in range(0 if first else 1, 1 if first else nb):
                    dst = (res * RESIDUE_PITCH, 1) if gathered[br] else (res + dil * WIN * n, dil)
                    window_block(br, dil, res, n, *dst)

    def merge_into(dst_ref):
        def merge(c, carry):
            def branch_rows(ref, br):
                if not gathered[br]:
                    return ref[br, pl.ds(pl.multiple_of(c * MERGE_ROWS, MERGE_ROWS), MERGE_ROWS), :]
                dil = BRANCHES[br][1]
                j0 = c * (MERGE_ROWS // dil)
                return jnp.concatenate(
                    [ref[br, pl.ds(j0 + i, dil, stride=RESIDUE_PITCH), :] for i in range(MERGE_ROWS // dil)],
                    axis=0)

            ms = [branch_rows(m_ref, br) for br in range(3)]
            m_all = jnp.maximum(jnp.maximum(ms[0], ms[1]), ms[2])
            ws = [jnp.exp2(mb - m_all) for mb in ms]
            num = sum(ws[br] * branch_rows(o_ref, br) for br in range(3))
            den = sum(ws[br] * branch_rows(l_ref, br) for br in range(3))
            out_rows = pl.ds(pl.multiple_of(c * MERGE_ROWS, MERGE_ROWS), MERGE_ROWS)
            dst_ref[out_rows, :] = (num / den).astype(dst_ref.dtype)
            return carry

        lax.fori_loop(0, S // MERGE_ROWS, merge, 0)

    return merge_into


def _attention(qkv, bias_rows):
    B, S, _ = qkv.shape
    n_br = len(BRANCHES)
    qkv_specs = [pl.BlockSpec((1, S, LANES),
                              lambda b, p, part=part, g=g: (b, 0, part * N_GROUPS + p * ATTN_UNITS + g))
                 for g in range(ATTN_UNITS) for part in range(3)]
    stat_rows = max(S, max(d for _, d in BRANCHES) * RESIDUE_PITCH)
    return pl.pallas_call(
        _attn_kernel,
        grid=(B, N_GROUPS // ATTN_UNITS),
        in_specs=qkv_specs + [pl.BlockSpec((n_br, N_HEADS, BIAS_ROW), lambda b, p: (0, 0, 0))],
        out_specs=pl.BlockSpec((1, S, ATTN_UNITS * LANES), lambda b, p: (b, 0, p)),
        out_shape=jax.ShapeDtypeStruct((B, S, D_ATTN), jnp.bfloat16),
        scratch_shapes=[pltpu.VMEM((n_br, N_HEADS, WIN, 2 * WIN), jnp.float32),
                        pltpu.VMEM((ATTN_UNITS, 3, S, LANES), jnp.float32)] +
                       [pltpu.VMEM((ATTN_UNITS, n_br, stat_rows, LANES), jnp.float32)] * 3,
        compiler_params=pltpu.CompilerParams(
            dimension_semantics=("arbitrary", "arbitrary"),
            vmem_limit_bytes=VMEM_LIMIT_BYTES),
        name="dilated_attn",
    )(*([qkv] * (3 * ATTN_UNITS)), bias_rows)


def _t5_causal_bucket(distance):
    max_exact = NUM_BUCKETS // 2
    d = jnp.maximum(distance, 1).astype(jnp.float32)
    large = max_exact + (jnp.log(d / max_exact) / math.log(MAX_DISTANCE / max_exact)
                         * (NUM_BUCKETS - max_exact)).astype(jnp.int32)
    large = jnp.minimum(large, NUM_BUCKETS - 1)
    return jnp.where(distance < max_exact, distance, large)


def _bias_rows(rel_bias):
    dist = 2 * WIN - jnp.arange(BIAS_ROW, dtype=jnp.int32)
    valid = (dist >= 0) & (dist <= WIN)
    rows = []
    for _, dil in BRANCHES:
        bucket = _t5_causal_bucket(jnp.clip(dist, 0, WIN) * dil)
        bias = jnp.zeros((N_HEADS, BIAS_ROW), jnp.float32)
        for b in range(NUM_BUCKETS):
            bias = jnp.where(bucket[None] == b, rel_bias[b].astype(jnp.float32)[:, None], bias)
        rows.append(jnp.where(valid[None], bias * LOG2E, NEG_INF))
    return jnp.stack(rows)


def _ffn_kernel(x_ref, attn_ref, u_ref, uh_ref, cw_ref, cb_ref, lng_ref, lnb_ref,
                wo_ref, g2_ref, w1_ref, w2_ref, gf_ref, out_ref, ubuf_ref, conv_ref, *, tiles_per_seq):
    tm = FFN_ROWS
    step = pl.program_id(0)
    last_step = pl.num_programs(0) - 1
    n_slabs = D_CONV // LANES
    n_ff = D_FF // FF_CHUNK
    n_chunks = tm // CONV_CHUNK
    off = CONV_HALO - (CONV_WIDTH - 1)

    def exact_zero(tile):
        half_word = jnp.uint32(16)
        bits = lax.shift_right_logical(pltpu.bitcast(tile, jnp.uint32), half_word)
        return pltpu.bitcast(lax.shift_right_logical(bits, half_word), jnp.float32)

    def conv_chunk(c):
        base = c * CONV_CHUNK
        slabs = []
        for s in range(n_slabs):
            cols = slice(s * LANES, (s + 1) * LANES)
            a = jnp.zeros((CONV_CHUNK, LANES), jnp.float32) + cb_ref[:, cols]
            for j in range(CONV_WIDTH):
                tap = ubuf_ref[s, pl.ds(base + off + j, CONV_CHUNK, stride=1), :]
                a = a + tap * cw_ref[j:j + 1, cols]
            slabs.append(a)
        y = jnp.concatenate(slabs, axis=-1)
        mu = jnp.mean(y, axis=-1, keepdims=True)
        cen = y - mu
        var = jnp.mean(cen * cen, axis=-1, keepdims=True)
        y = cen * lax.rsqrt(var + LN_EPS) * lng_ref[...] + lnb_ref[...]
        y = y * jax.nn.sigmoid(y)
        conv_ref[base:base + CONV_CHUNK, :] = y.astype(jnp.bfloat16)
        fold = y[0:8, 0:LANES]
        for r in range(CONV_CHUNK // 8):
            for s in range(n_slabs):
                if r or s:
                    fold = jnp.maximum(fold, y[8 * r:8 * r + 8, s * LANES:(s + 1) * LANES])
        return exact_zero(fold)

    def step_body(ffn_half, conv_half):
        if ffn_half:
            mixed = jnp.concatenate([attn_ref[0], conv_ref[...]], axis=-1)
            x1 = x_ref[0] + jnp.dot(mixed, wo_ref[...], preferred_element_type=jnp.float32)
            inv = lax.rsqrt(jnp.mean(x1 * x1, axis=-1, keepdims=True) + RMS_EPS)
            h = (x1 * inv * g2_ref[...]).astype(jnp.bfloat16)
        if conv_half:
            seq_start = (step % tiles_per_seq) == 0
            ubuf_ref[:, 0:CONV_HALO, :] = jnp.where(seq_start, 0.0, uh_ref[0])
            ubuf_ref[:, CONV_HALO:CONV_HALO + tm, :] = u_ref[0]
        if not ffn_half:
            for c in range(n_chunks):
                conv_chunk(c)
            return

        chunk_ids = iter(range(n_chunks))
        anchor_cols = FF_CHUNK * n_ff // n_chunks
        assert anchor_cols % LANES == 0

        def anchored(d):
            top = []
            for g in range(d.shape[1] // anchor_cols):
                top.append(d[0:8, anchor_cols * g:anchor_cols * g + LANES] + conv_chunk(next(chunk_ids)))
                if anchor_cols > LANES:
                    top.append(d[0:8, anchor_cols * g + LANES:anchor_cols * (g + 1)])
            return jnp.concatenate([jnp.concatenate(top, axis=1), d[8:]], axis=0)

        acc = x1
        for c in range(n_ff):
            cols = slice(c * FF_CHUNK, (c + 1) * FF_CHUNK)
            d = jnp.dot(h, w1_ref[:, cols], preferred_element_type=jnp.float32)
            f = jnp.maximum(anchored(d) if conv_half else d, 0.0)
            f = (f * f).astype(jnp.bfloat16)
            acc = acc + jnp.dot(f, w2_ref[cols, :], preferred_element_type=jnp.float32)
        inv = lax.rsqrt(jnp.mean(acc * acc, axis=-1, keepdims=True) + RMS_EPS)
        out_ref[0] = acc * inv * gf_ref[...]

    pl.when(step == 0)(lambda: step_body(False, True))
    pl.when((step > 0) & (step < last_step))(lambda: step_body(True, True))
    pl.when(step == last_step)(lambda: step_body(True, False))


def _ffn(x, attn, u, conv_w, conv_b, ln_g, ln_b, w_o, g2, w1, w2, gf):
    B, S, _ = x.shape
    tm = FFN_ROWS
    tiles_per_seq = S // tm
    n_tiles = B * tiles_per_seq
    halo_blocks = tm // CONV_HALO
    x = x.reshape(n_tiles, tm, D_MODEL)
    attn = attn.reshape(n_tiles, tm, D_ATTN)

    def ffn_tile(s):
        return (jnp.maximum(s - 1, 0), 0, 0)

    def conv_tile(s):
        t = jnp.minimum(s, n_tiles - 1)
        return t // tiles_per_seq, t % tiles_per_seq

    def u_block(s):
        b, i = conv_tile(s)
        return (b, 0, i, 0)

    def halo_block(s):
        b, i = conv_tile(s)
        return (b, 0, jnp.maximum(i * halo_blocks - 1, 0), 0)

    const = lambda s: (0, 0)
    resident = functools.partial(pl.BlockSpec, index_map=const, pipeline_mode=pl.Buffered(1))
    out = pl.pallas_call(
        functools.partial(_ffn_kernel, tiles_per_seq=tiles_per_seq),
        grid=(n_tiles + 1,),
        in_specs=[
            pl.BlockSpec((1, tm, D_MODEL), ffn_tile),
            pl.BlockSpec((1, tm, D_ATTN), ffn_tile),
            pl.BlockSpec((1, D_CONV // LANES, tm, LANES), u_block),
            pl.BlockSpec((1, D_CONV // LANES, CONV_HALO, LANES), halo_block),
            pl.BlockSpec((CONV_WIDTH, D_CONV), const),
            pl.BlockSpec((1, D_CONV), const),
            pl.BlockSpec((1, D_CONV), const),
            pl.BlockSpec((1, D_CONV), const),
            resident((D_MODEL, D_MODEL)),
            pl.BlockSpec((1, D_MODEL), const),
            resident((D_MODEL, D_FF)),
            resident((D_FF, D_MODEL)),
            pl.BlockSpec((1, D_MODEL), const),
        ],
        out_specs=pl.BlockSpec((1, tm, D_MODEL), ffn_tile),
        out_shape=jax.ShapeDtypeStruct((n_tiles, tm, D_MODEL), jnp.float32),
        scratch_shapes=[
            pltpu.VMEM((D_CONV // LANES, CONV_HALO + tm, LANES), jnp.float32),
            pltpu.VMEM((tm, D_CONV), jnp.bfloat16),
        ],
        compiler_params=pltpu.CompilerParams(
            dimension_semantics=("arbitrary",),
            vmem_limit_bytes=VMEM_LIMIT_BYTES),
        name="conv_outproj_ffn",
    )(x, attn, u, u, conv_w, conv_b, ln_g, ln_b, w_o, g2, w1, w2, gf)
    return out.reshape(B, S, D_MODEL)


def kernel(x, norm1_g, w_in, conv_w, conv_b, conv_ln_g, conv_ln_b, w_o, norm2_g, w_ff1, w_ff2,
           rel_bias, final_g):
    assert norm1_g.shape[0] == 1, "single-layer problem"
    qkv, u, w_o_bf, w1_bf, w2_bf = _project(x, norm1_g, w_in[0], w_o[0], w_ff1[0], w_ff2[0])
    attn = _attention(qkv, _bias_rows(rel_bias))
    return _ffn(x, attn, u, conv_w[0], conv_b, conv_ln_g, conv_ln_b, w_o_bf, norm2_g,
                w1_bf, w2_bf, final_g.reshape(1, D_MODEL))
```

```python
import functools
import math

import jax
import jax.numpy as jnp
from jax import lax
from jax.experimental import pallas as pl
from jax.experimental.pallas import tpu as pltpu

D_MODEL = 1024
N_HEADS = 8
HEAD_DIM = 64
D_ATTN = N_HEADS * HEAD_DIM
D_CONV = D_MODEL - D_ATTN
D_QKV = 3 * D_ATTN
D_IN_PROJ = D_QKV + 2 * D_CONV
CONV_WIDTH = 31
D_FF = 4 * D_MODEL
BRANCHES = ((128, 1), (512, 4), (2048, 16))
WIN = 128
NUM_BUCKETS = 32
MAX_DISTANCE = 2048
RMS_EPS = 1e-6
LN_EPS = 1e-5
NEG_INF = -1e30
LOG2E = math.log2(math.e)

LANES = 128
BF16_TILE_ROWS = 16
PROJ_COLS = 256
HEADS_PER_GROUP = LANES // HEAD_DIM
N_GROUPS = N_HEADS // HEADS_PER_GROUP
VMEM_LIMIT_BYTES = 56 * 1024 * 1024

PROJ_ROWS = 1024
CONV_HALO = 32
CONV_CHUNK = 32
SUB_DIL = 4
RESIDUE_PITCH = 136
ATTN_UNITS = 2
MERGE_ROWS = 512
BIAS_ROW = 4 * WIN
FFN_ROWS = 512
FF_CHUNK = 512


def _proj_kernel(x_ref, g1_ref, win_f32_ref, wo_ref, w1_ref, w2_ref,
                 qkv_ref, u_ref, wo_bf_ref, w1_bf_ref, w2_bf_ref, win_ref):
    @pl.when((pl.program_id(0) == 0) & (pl.program_id(1) == 0))
    def _():
        win_ref[...] = win_f32_ref[...].astype(jnp.bfloat16)

    wo_bf_ref[...] = wo_ref[...].astype(jnp.bfloat16)
    w1_bf_ref[...] = w1_ref[...].astype(jnp.bfloat16)
    w2_bf_ref[...] = w2_ref[...].astype(jnp.bfloat16)

    x = x_ref[0]
    inv = lax.rsqrt(jnp.mean(x * x, axis=-1, keepdims=True) + RMS_EPS)
    h = (x * inv * g1_ref[...]).astype(jnp.bfloat16)

    slabs_per_group = PROJ_COLS // LANES
    for g in range(D_CONV // PROJ_COLS):
        a_cols = slice(D_QKV + g * PROJ_COLS, D_QKV + (g + 1) * PROJ_COLS)
        gate_cols = slice(D_QKV + D_CONV + g * PROJ_COLS, D_QKV + D_CONV + (g + 1) * PROJ_COLS)
        a = jnp.dot(h, win_ref[:, a_cols], preferred_element_type=jnp.float32)
        gate = jnp.dot(h, win_ref[:, gate_cols], preferred_element_type=jnp.float32)
        u = a * jax.nn.sigmoid(gate)
        for k in range(slabs_per_group):
            u_ref[0, slabs_per_group * g + k] = u[:, k * LANES:(k + 1) * LANES]

    q_scale = LOG2E / math.sqrt(HEAD_DIM)
    for g in range(D_QKV // PROJ_COLS):
        cols = slice(g * PROJ_COLS, (g + 1) * PROJ_COLS)
        z = jnp.dot(h, win_ref[:, cols], preferred_element_type=jnp.float32)
        if g * PROJ_COLS < D_ATTN:
            z = z * q_scale
        qkv_ref[0, :, cols] = z


def _project(x, g1, w_in, w_o, w1, w2):
    B, S, _ = x.shape
    tm = PROJ_ROWS
    tiles_per_seq = S // tm
    n_steps = B * tiles_per_seq
    const = lambda b, i: (0, 0)
    step_rows = lambda b, i: (b * tiles_per_seq + i, 0)

    def weight_slice(w):
        rows = w.shape[0] // n_steps
        assert rows * n_steps == w.shape[0] and rows % BF16_TILE_ROWS == 0
        return pl.BlockSpec((rows, w.shape[1]), step_rows)

    weights = (w_o, w1, w2)
    return pl.pallas_call(
        _proj_kernel,
        grid=(B, tiles_per_seq),
        in_specs=[
            pl.BlockSpec((1, tm, D_MODEL), lambda b, i: (b, i, 0)),
            pl.BlockSpec((1, D_MODEL), const),
            pl.BlockSpec((D_MODEL, D_IN_PROJ), const, pipeline_mode=pl.Buffered(1)),
        ] + [weight_slice(w) for w in weights],
        out_specs=[
            pl.BlockSpec((1, tm, D_QKV), lambda b, i: (b, i, 0)),
            pl.BlockSpec((1, D_CONV // LANES, tm, LANES), lambda b, i: (b, 0, i, 0)),
        ] + [weight_slice(w) for w in weights],
        out_shape=[
            jax.ShapeDtypeStruct((B, S, D_QKV), jnp.float32),
            jax.ShapeDtypeStruct((B, D_CONV // LANES, S, LANES), jnp.float32),
        ] + [jax.ShapeDtypeStruct(w.shape, jnp.bfloat16) for w in weights],
        scratch_shapes=[pltpu.VMEM((D_MODEL, D_IN_PROJ), jnp.bfloat16)],
        compiler_params=pltpu.CompilerParams(
            dimension_semantics=("arbitrary", "arbitrary"),
            vmem_limit_bytes=VMEM_LIMIT_BYTES),
        name="in_proj",
    )(x, g1, w_in, *weights)


def _attn_kernel(*refs):
    assert HEADS_PER_GROUP == 2
    qkv_refs = refs[:3 * ATTN_UNITS]
    bias_rows_ref, out_ref, bias_ref, sub_all_ref, o_all_ref, l_all_ref, m_all_ref = refs[3 * ATTN_UNITS:]
    S = out_ref.shape[1]

    @pl.when((pl.program_id(0) == 0) & (pl.program_id(1) == 0))
    def _():
        for br in range(len(BRANCHES)):
            for h in range(N_HEADS):
                row = jnp.broadcast_to(bias_rows_ref[br, h:h + 1, :], (WIN, BIAS_ROW))
                table = pltpu.roll(row, BIAS_ROW - WIN, 1, stride=1, stride_axis=0)
                bias_ref[br, h] = table[:, :2 * WIN]

    merged = [_attn_unit_blocks(qkv_refs[3 * g:3 * g + 3], bias_ref,
                                (pl.program_id(1) * ATTN_UNITS + g) * HEADS_PER_GROUP, S,
                                sub_all_ref.at[g], o_all_ref.at[g], l_all_ref.at[g], m_all_ref.at[g])
              for g in range(ATTN_UNITS)]

    def merge(c, carry):
        rows = pl.ds(pl.multiple_of(c * MERGE_ROWS, MERGE_ROWS), MERGE_ROWS)
        for g in range(ATTN_UNITS):
            out_ref[0, rows, g * LANES:(g + 1) * LANES] = merged[g](c).astype(out_ref.dtype)
        return carry

    lax.fori_loop(0, S // MERGE_ROWS, merge, 0)


def _attn_unit_blocks(qkv_refs, bias_ref, first_head, S, sub_ref, o_ref, l_ref, m_ref):
    nat_refs = tuple(r.at[0] for r in qkv_refs)
    sub_len = S // SUB_DIL

    for a in range(3):
        for r in range(SUB_DIL):
            sub_ref[a, r * sub_len:(r + 1) * sub_len, :] = nat_refs[a][pl.ds(r, sub_len, stride=SUB_DIL), :]

    def load_rows(a, dil, res, j0, n):
        if dil == 1:
            x = nat_refs[a][pl.ds(j0, n), :]
        else:
            step = dil // SUB_DIL
            start = (res % SUB_DIL) * sub_len + step * j0 + res // SUB_DIL
            x = sub_ref[a, pl.ds(start, n), :] if step == 1 else sub_ref[a, pl.ds(start, n, stride=step), :]
        return x.astype(jnp.bfloat16)

    lane = lax.broadcasted_iota(jnp.int32, (1, LANES), 1)
    head_lanes = [(lane >= h * HEAD_DIM) & (lane < (h + 1) * HEAD_DIM) for h in range(HEADS_PER_GROUP)]
    head_bits = [jnp.where(hl, jnp.uint32(0xFFFFFFFF), jnp.uint32(0)) for hl in head_lanes]
    head_ones = [jnp.where(hl, 1.0, 0.0).astype(jnp.bfloat16) for hl in head_lanes]

    def keep_head(x, h):
        return pltpu.bitcast(pltpu.bitcast(x, jnp.uint32) & head_bits[h], jnp.bfloat16)

    def window_block(br, dil, res, n, dst_start, dst_stride):
        first = n == 0
        nk = WIN if first else 2 * WIN
        k0 = n * WIN if first else (n - 1) * WIN
        q2 = load_rows(0, dil, res, n * WIN, WIN)
        k2 = load_rows(1, dil, res, k0, nk)
        v2 = load_rows(2, dil, res, k0, nk)
        q_both = jnp.concatenate([keep_head(q2, h) for h in range(HEADS_PER_GROUP)], axis=0)
        s = lax.dot_general(q_both, k2, (((1,), (1,)), ((), ())), preferred_element_type=jnp.float32)
        bias = bias_ref[br, pl.ds(first_head, HEADS_PER_GROUP), :, 2 * WIN - nk:2 * WIN]
        s = s + bias.reshape(HEADS_PER_GROUP * WIN, nk)
        m = jnp.max(s, axis=-1, keepdims=True)
        p = jnp.exp2(s - m).astype(jnp.bfloat16)
        p_cat = jnp.concatenate([p[h * WIN:(h + 1) * WIN] for h in range(HEADS_PER_GROUP)], axis=1)
        v_cat = jnp.concatenate(
            [jnp.concatenate([keep_head(v2, h), jnp.broadcast_to(head_ones[h], (nk, LANES))], axis=1)
             for h in range(HEADS_PER_GROUP)], axis=0)
        r = jnp.dot(p_cat, v_cat, preferred_element_type=jnp.float32)
        o_ref[br, rows(dst_start, dst_stride), :] = r[:, :LANES]
        l_ref[br, rows(dst_start, dst_stride), :] = r[:, LANES:]
        m_ref[br, rows(dst_start, dst_stride), :] = jnp.where(head_lanes[0], m[:WIN], m[WIN:])

    def rows(start, stride):
        return pl.ds(start, WIN) if stride == 1 else pl.ds(start, WIN, stride=stride)

    gathered = []
    for br, (_, dil) in enumerate(BRANCHES):
        assert dil == 1 or dil % SUB_DIL == 0
        nb = S // dil // WIN
        gathered.append(dil % 8 == 0)
        assert not gathered[br] or nb == 1
        for first in (True, False):
            for res in range(dil):
                for n in range(0 if first else 1, 1 if first else nb):
                    dst = (res * RESIDUE_PITCH, 1) if gathered[br] else (res + dil * WIN * n, dil)
                    window_block(br, dil, res, n, *dst)

    def merged_rows(c):
        def branch_rows(ref, br):
            if not gathered[br]:
                return ref[br, pl.ds(pl.multiple_of(c * MERGE_ROWS, MERGE_ROWS), MERGE_ROWS), :]
            dil = BRANCHES[br][1]
            j0 = c * (MERGE_ROWS // dil)
            return jnp.concatenate(
                [ref[br, pl.ds(j0 + i, dil, stride=RESIDUE_PITCH), :] for i in range(MERGE_ROWS // dil)],
                axis=0)

        ms = [branch_rows(m_ref, br) for br in range(3)]
        m_all = jnp.maximum(jnp.maximum(ms[0], ms[1]), ms[2])
        ws = [jnp.exp2(mb - m_all) for mb in ms]
        num = sum(ws[br] * branch_rows(o_ref, br) for br in range(3))
        den = sum(ws[br] * branch_rows(l_ref, br) for br in range(3))
        return num / den

    return merged_rows


def _attention(qkv, bias_rows):
    B, S, _ = qkv.shape
    n_br = len(BRANCHES)
    qkv_specs = [pl.BlockSpec((1, S, LANES),
                              lambda b, p, part=part, g=g: (b, 0, part * N_GROUPS + p * ATTN_UNITS + g))
                 for g in range(ATTN_UNITS) for part in range(3)]
    stat_rows = max(S, max(d for _, d in BRANCHES) * RESIDUE_PITCH)
    return pl.pallas_call(
        _attn_kernel,
        grid=(B, N_GROUPS // ATTN_UNITS),
        in_specs=qkv_specs + [pl.BlockSpec((n_br, N_HEADS, BIAS_ROW), lambda b, p: (0, 0, 0))],
        out_specs=pl.BlockSpec((1, S, ATTN_UNITS * LANES), lambda b, p: (b, 0, p)),
        out_shape=jax.ShapeDtypeStruct((B, S, D_ATTN), jnp.bfloat16),
        scratch_shapes=[pltpu.VMEM((n_br, N_HEADS, WIN, 2 * WIN), jnp.float32),
                        pltpu.VMEM((ATTN_UNITS, 3, S, LANES), jnp.float32)] +
                       [pltpu.VMEM((ATTN_UNITS, n_br, stat_rows, LANES), jnp.float32)] * 3,
        compiler_params=pltpu.CompilerParams(
            dimension_semantics=("arbitrary", "arbitrary"),
            vmem_limit_bytes=VMEM_LIMIT_BYTES),
        name="dilated_attn",
    )(*([qkv] * (3 * ATTN_UNITS)), bias_rows)


def _t5_causal_bucket(distance):
    max_exact = NUM_BUCKETS // 2
    d = jnp.maximum(distance, 1).astype(jnp.float32)
    large = max_exact + (jnp.log(d / max_exact) / math.log(MAX_DISTANCE / max_exact)
                         * (NUM_BUCKETS - max_exact)).astype(jnp.int32)
    large = jnp.minimum(large, NUM_BUCKETS - 1)
    return jnp.where(distance < max_exact, distance, large)


def _bias_rows(rel_bias):
    dist = 2 * WIN - jnp.arange(BIAS_ROW, dtype=jnp.int32)
    valid = (dist >= 0) & (dist <= WIN)
    rows = []
    for _, dil in BRANCHES:
        bucket = _t5_causal_bucket(jnp.clip(dist, 0, WIN) * dil)
        bias = jnp.zeros((N_HEADS, BIAS_ROW), jnp.float32)
        for b in range(NUM_BUCKETS):
            bias = jnp.where(bucket[None] == b, rel_bias[b].astype(jnp.float32)[:, None], bias)
        rows.append(jnp.where(valid[None], bias * LOG2E, NEG_INF))
    return jnp.stack(rows)


def _ffn_kernel(x_ref, attn_ref, u_ref, uh_ref, cw_ref, cb_ref, lng_ref, lnb_ref,
                wo_ref, g2_ref, w1_ref, w2_ref, gf_ref, out_ref, ubuf_ref, conv_ref, *, tiles_per_seq):
    tm = FFN_ROWS
    step = pl.program_id(0)
    last_step = pl.num_programs(0) - 1
    n_slabs = D_CONV // LANES
    n_ff = D_FF // FF_CHUNK
    n_chunks = tm // CONV_CHUNK
    off = CONV_HALO - (CONV_WIDTH - 1)

    def exact_zero(tile):
        half_word = jnp.uint32(16)
        bits = lax.shift_right_logical(pltpu.bitcast(tile, jnp.uint32), half_word)
        return pltpu.bitcast(lax.shift_right_logical(bits, half_word), jnp.float32)

    def conv_chunk(c):
        base = c * CONV_CHUNK
        slabs = []
        for s in range(n_slabs):
            cols = slice(s * LANES, (s + 1) * LANES)
            a = jnp.zeros((CONV_CHUNK, LANES), jnp.float32) + cb_ref[:, cols]
            for j in range(CONV_WIDTH):
                tap = ubuf_ref[s, pl.ds(base + off + j, CONV_CHUNK, stride=1), :]
                a = a + tap * cw_ref[j:j + 1, cols]
            slabs.append(a)
        y = jnp.concatenate(slabs, axis=-1)
        mu = jnp.mean(y, axis=-1, keepdims=True)
        cen = y - mu
        var = jnp.mean(cen * cen, axis=-1, keepdims=True)
        y = cen * lax.rsqrt(var + LN_EPS) * lng_ref[...] + lnb_ref[...]
        y = y * jax.nn.sigmoid(y)
        conv_ref[base:base + CONV_CHUNK, :] = y.astype(jnp.bfloat16)
        fold = y[0:8, 0:LANES]
        for r in range(CONV_CHUNK // 8):
            for s in range(n_slabs):
                if r or s:
                    fold = jnp.maximum(fold, y[8 * r:8 * r + 8, s * LANES:(s + 1) * LANES])
        return exact_zero(fold)

    def step_body(ffn_half, conv_half):
        if ffn_half:
            mixed = jnp.concatenate([attn_ref[0], conv_ref[...]], axis=-1)
            x1 = x_ref[0] + jnp.dot(mixed, wo_ref[...], preferred_element_type=jnp.float32)
            inv = lax.rsqrt(jnp.mean(x1 * x1, axis=-1, keepdims=True) + RMS_EPS)
            h = (x1 * inv * g2_ref[...]).astype(jnp.bfloat16)
        if conv_half:
            seq_start = (step % tiles_per_seq) == 0
            ubuf_ref[:, 0:CONV_HALO, :] = jnp.where(seq_start, 0.0, uh_ref[0])
            ubuf_ref[:, CONV_HALO:CONV_HALO + tm, :] = u_ref[0]
        if not ffn_half:
            for c in range(n_chunks):
                conv_chunk(c)
            return

        chunk_ids = iter(range(n_chunks))
        anchor_cols = FF_CHUNK * n_ff // n_chunks
        assert anchor_cols % LANES == 0

        def anchored(d):
            top = []
            for g in range(d.shape[1] // anchor_cols):
                top.append(d[0:8, anchor_cols * g:anchor_cols * g + LANES] + conv_chunk(next(chunk_ids)))
                if anchor_cols > LANES:
                    top.append(d[0:8, anchor_cols * g + LANES:anchor_cols * (g + 1)])
            return jnp.concatenate([jnp.concatenate(top, axis=1), d[8:]], axis=0)

        acc = x1
        for c in range(n_ff):
            cols = slice(c * FF_CHUNK, (c + 1) * FF_CHUNK)
            d = jnp.dot(h, w1_ref[:, cols], preferred_element_type=jnp.float32)
            f = jnp.maximum(anchored(d) if conv_half else d, 0.0)
            f = (f * f).astype(jnp.bfloat16)
            acc = acc + jnp.dot(f, w2_ref[cols, :], preferred_element_type=jnp.float32)
        inv = lax.rsqrt(jnp.mean(acc * acc, axis=-1, keepdims=True) + RMS_EPS)
        out_ref[0] = acc * inv * gf_ref[...]

    pl.when(step == 0)(lambda: step_body(False, True))
    pl.when((step > 0) & (step < last_step))(lambda: step_body(True, True))
    pl.when(step == last_step)(lambda: step_body(True, False))


def _ffn(x, attn, u, conv_w, conv_b, ln_g, ln_b, w_o, g2, w1, w2, gf):
    B, S, _ = x.shape
    tm = FFN_ROWS
    tiles_per_seq = S // tm
    n_tiles = B * tiles_per_seq
    halo_blocks = tm // CONV_HALO
    x = x.reshape(n_tiles, tm, D_MODEL)
    attn = attn.reshape(n_tiles, tm, D_ATTN)

    def ffn_tile(s):
        return (jnp.maximum(s - 1, 0), 0, 0)

    def conv_tile(s):
        t = jnp.minimum(s, n_tiles - 1)
        return t // tiles_per_seq, t % tiles_per_seq

    def u_block(s):
        b, i = conv_tile(s)
        return (b, 0, i, 0)

    def halo_block(s):
        b, i = conv_tile(s)
        return (b, 0, jnp.maximum(i * halo_blocks - 1, 0), 0)

    const = lambda s: (0, 0)
    resident = functools.partial(pl.BlockSpec, index_map=const, pipeline_mode=pl.Buffered(1))
    out = pl.pallas_call(
        functools.partial(_ffn_kernel, tiles_per_seq=tiles_per_seq),
        grid=(n_tiles + 1,),
        in_specs=[
            pl.BlockSpec((1, tm, D_MODEL), ffn_tile),
            pl.BlockSpec((1, tm, D_ATTN), ffn_tile),
            pl.BlockSpec((1, D_CONV // LANES, tm, LANES), u_block),
            pl.BlockSpec((1, D_CONV // LANES, CONV_HALO, LANES), halo_block),
            pl.BlockSpec((CONV_WIDTH, D_CONV), const),
            pl.BlockSpec((1, D_CONV), const),
            pl.BlockSpec((1, D_CONV), const),
            pl.BlockSpec((1, D_CONV), const),
            resident((D_MODEL, D_MODEL)),
            pl.BlockSpec((1, D_MODEL), const),
            resident((D_MODEL, D_FF)),
            resident((D_FF, D_MODEL)),
            pl.BlockSpec((1, D_MODEL), const),
        ],
        out_specs=pl.BlockSpec((1, tm, D_MODEL), ffn_tile),
        out_shape=jax.ShapeDtypeStruct((n_tiles, tm, D_MODEL), jnp.float32),
        scratch_shapes=[
            pltpu.VMEM((D_CONV // LANES, CONV_HALO + tm, LANES), jnp.float32),
            pltpu.VMEM((tm, D_CONV), jnp.bfloat16),
        ],
        compiler_params=pltpu.CompilerParams(
            dimension_semantics=("arbitrary",),
            vmem_limit_bytes=VMEM_LIMIT_BYTES),
        name="conv_outproj_ffn",
    )(x, attn, u, u, conv_w, conv_b, ln_g, ln_b, w_o, g2, w1, w2, gf)
    return out.reshape(B, S, D_MODEL)


def kernel(x, norm1_g, w_in, conv_w, conv_b, conv_ln_g, conv_ln_b, w_o, norm2_g, w_ff1, w_ff2,
           rel_bias, final_g):
    assert norm1_g.shape[0] == 1, "single-layer problem"
    qkv, u, w_o_bf, w1_bf, w2_bf = _project(x, norm1_g, w_in[0], w_o[0], w_ff1[0], w_ff2[0])
    attn = _attention(qkv, _bias_rows(rel_bias))
    return _ffn(x, attn, u, conv_w[0], conv_b, conv_ln_g, conv_ln_b, w_o_bf, norm2_g,
                w1_bf, w2_bf, final_g.reshape(1, D_MODEL))
```

```python
import functools
import math

import jax
import jax.numpy as jnp
from jax import lax
from jax.experimental import pallas as pl
from jax.experimental.pallas import tpu as pltpu

D_MODEL = 1024
N_HEADS = 8
HEAD_DIM = 64
D_ATTN = N_HEADS * HEAD_DIM
D_CONV = D_MODEL - D_ATTN
D_QKV = 3 * D_ATTN
D_IN_PROJ = D_QKV + 2 * D_CONV
CONV_WIDTH = 31
D_FF = 4 * D_MODEL
BRANCHES = ((128, 1), (512, 4), (2048, 16))
WIN = 128
NUM_BUCKETS = 32
MAX_DISTANCE = 2048
RMS_EPS = 1e-6
LN_EPS = 1e-5
NEG_INF = -1e30
LOG2E = math.log2(math.e)

LANES = 128
BF16_TILE_ROWS = 16
PROJ_COLS = 256
HEADS_PER_GROUP = LANES // HEAD_DIM
N_GROUPS = N_HEADS // HEADS_PER_GROUP
VMEM_LIMIT_BYTES = 56 * 1024 * 1024

PROJ_ROWS = 1024
CONV_HALO = 32
CONV_CHUNK = 32
SUB_DIL = 4
RESIDUE_PITCH = 136
ATTN_UNITS = 2
MERGE_ROWS = 512
BIAS_ROW = 4 * WIN
FFN_ROWS = 512
FF_CHUNK = 512


def _proj_kernel(x_ref, g1_ref, win_f32_ref, wo_ref, w1_ref, w2_ref,
                 qkv_ref, u_ref, wo_bf_ref, w1_bf_ref, w2_bf_ref, win_ref):
    @pl.when((pl.program_id(0) == 0) & (pl.program_id(1) == 0))
    def _():
        win_ref[...] = win_f32_ref[...].astype(jnp.bfloat16)

    wo_bf_ref[...] = wo_ref[...].astype(jnp.bfloat16)
    w1_bf_ref[...] = w1_ref[...].astype(jnp.bfloat16)
    w2_bf_ref[...] = w2_ref[...].astype(jnp.bfloat16)

    x = x_ref[0]
    inv = lax.rsqrt(jnp.mean(x * x, axis=-1, keepdims=True) + RMS_EPS)
    h = (x * inv * g1_ref[...]).astype(jnp.bfloat16)

    slabs_per_group = PROJ_COLS // LANES
    for g in range(D_CONV // PROJ_COLS):
        a_cols = slice(D_QKV + g * PROJ_COLS, D_QKV + (g + 1) * PROJ_COLS)
        gate_cols = slice(D_QKV + D_CONV + g * PROJ_COLS, D_QKV + D_CONV + (g + 1) * PROJ_COLS)
        a = jnp.dot(h, win_ref[:, a_cols], preferred_element_type=jnp.float32)
        gate = jnp.dot(h, win_ref[:, gate_cols], preferred_element_type=jnp.float32)
        u = a * jax.nn.sigmoid(gate)
        for k in range(slabs_per_group):
            u_ref[0, slabs_per_group * g + k] = u[:, k * LANES:(k + 1) * LANES]

    q_scale = LOG2E / math.sqrt(HEAD_DIM)
    for g in range(D_QKV // PROJ_COLS):
        cols = slice(g * PROJ_COLS, (g + 1) * PROJ_COLS)
        z = jnp.dot(h, win_ref[:, cols], preferred_element_type=jnp.float32)
        if g * PROJ_COLS < D_ATTN:
            z = z * q_scale
        qkv_ref[0, :, cols] = z


def _project(x, g1, w_in, w_o, w1, w2):
    B, S, _ = x.shape
    tm = PROJ_ROWS
    tiles_per_seq = S // tm
    n_steps = B * tiles_per_seq
    const = lambda b, i: (0, 0)
    step_rows = lambda b, i: (b * tiles_per_seq + i, 0)

    def weight_slice(w):
        rows = w.shape[0] // n_steps
        assert rows * n_steps == w.shape[0] and rows % BF16_TILE_ROWS == 0
        return pl.BlockSpec((rows, w.shape[1]), step_rows)

    weights = (w_o, w1, w2)
    return pl.pallas_call(
        _proj_kernel,
        grid=(B, tiles_per_seq),
        in_specs=[
            pl.BlockSpec((1, tm, D_MODEL), lambda b, i: (b, i, 0)),
            pl.BlockSpec((1, D_MODEL), const),
            pl.BlockSpec((D_MODEL, D_IN_PROJ), const, pipeline_mode=pl.Buffered(1)),
        ] + [weight_slice(w) for w in weights],
        out_specs=[
            pl.BlockSpec((1, tm, D_QKV), lambda b, i: (b, i, 0)),
            pl.BlockSpec((1, D_CONV // LANES, tm, LANES), lambda b, i: (b, 0, i, 0)),
        ] + [weight_slice(w) for w in weights],
        out_shape=[
            jax.ShapeDtypeStruct((B, S, D_QKV), jnp.float32),
            jax.ShapeDtypeStruct((B, D_CONV // LANES, S, LANES), jnp.float32),
        ] + [jax.ShapeDtypeStruct(w.shape, jnp.bfloat16) for w in weights],
        scratch_shapes=[pltpu.VMEM((D_MODEL, D_IN_PROJ), jnp.bfloat16)],
        compiler_params=pltpu.CompilerParams(
            dimension_semantics=("arbitrary", "arbitrary"),
            vmem_limit_bytes=VMEM_LIMIT_BYTES),
        name="in_proj",
    )(x, g1, w_in, *weights)


def _attn_kernel(*refs):
    assert HEADS_PER_GROUP == 2
    qkv_refs = refs[:3 * ATTN_UNITS]
    bias_rows_ref, out_ref, bias_ref, sub_all_ref, o_all_ref, l_all_ref, m_all_ref = refs[3 * ATTN_UNITS:]
    S = out_ref.shape[1]

    @pl.when((pl.program_id(0) == 0) & (pl.program_id(1) == 0))
    def _():
        for br in range(len(BRANCHES)):
            for h in range(N_HEADS):
                row = jnp.broadcast_to(bias_rows_ref[br, h:h + 1, :], (WIN, BIAS_ROW))
                table = pltpu.roll(row, BIAS_ROW - WIN, 1, stride=1, stride_axis=0)
                bias_ref[br, h] = table[:, :2 * WIN]

    merged = [_attn_unit_blocks(qkv_refs[3 * g:3 * g + 3], bias_ref,
                                (pl.program_id(1) * ATTN_UNITS + g) * HEADS_PER_GROUP, S,
                                sub_all_ref.at[g], o_all_ref.at[g], l_all_ref.at[g], m_all_ref.at[g])
              for g in range(ATTN_UNITS)]

    def merge(c, carry):
        rows = pl.ds(pl.multiple_of(c * MERGE_ROWS, MERGE_ROWS), MERGE_ROWS)
        for g in range(ATTN_UNITS):
            out_ref[0, rows, g * LANES:(g + 1) * LANES] = merged[g](c).astype(out_ref.dtype)
        return carry

    lax.fori_loop(0, S // MERGE_ROWS, merge, 0)


def _attn_unit_blocks(qkv_refs, bias_ref, first_head, S, sub_ref, o_ref, l_ref, m_ref):
    nat_refs = tuple(r.at[0] for r in qkv_refs)
    sub_len = S // SUB_DIL

    for a in range(3):
        for r in range(SUB_DIL):
            sub_ref[a, r * sub_len:(r + 1) * sub_len, :] = nat_refs[a][pl.ds(r, sub_len, stride=SUB_DIL), :]

    def load_rows(a, dil, res, j0, n):
        if dil == 1:
            x = nat_refs[a][pl.ds(j0, n), :]
        else:
            step = dil // SUB_DIL
            start = (res % SUB_DIL) * sub_len + step * j0 + res // SUB_DIL
            x = sub_ref[a, pl.ds(start, n), :] if step == 1 else sub_ref[a, pl.ds(start, n, stride=step), :]
        return x.astype(jnp.bfloat16)

    lane = lax.broadcasted_iota(jnp.int32, (1, LANES), 1)
    head_lanes = [(lane >= h * HEAD_DIM) & (lane < (h + 1) * HEAD_DIM) for h in range(HEADS_PER_GROUP)]
    head_bits = [jnp.where(hl, jnp.uint32(0xFFFFFFFF), jnp.uint32(0)) for hl in head_lanes]
    head_ones = [jnp.where(hl, 1.0, 0.0).astype(jnp.bfloat16) for hl in head_lanes]

    def keep_head(x, h):
        return pltpu.bitcast(pltpu.bitcast(x, jnp.uint32) & head_bits[h], jnp.bfloat16)

    def window_block(br, dil, res, n, dst_start, dst_stride):
        first = n == 0
        nk = WIN if first else 2 * WIN
        k0 = n * WIN if first else (n - 1) * WIN
        q2 = load_rows(0, dil, res, n * WIN, WIN)
        k2 = load_rows(1, dil, res, k0, nk)
        v2 = load_rows(2, dil, res, k0, nk)
        contract_cols = (((1,), (1,)), ((), ()))
        if first:
            k_both = jnp.concatenate([keep_head(k2, h) for h in range(HEADS_PER_GROUP)], axis=0)
            s = lax.dot_general(q2, k_both, contract_cols, preferred_element_type=jnp.float32)
            s = s + jnp.concatenate([bias_ref[br, first_head + h, :, WIN:2 * WIN]
                                     for h in range(HEADS_PER_GROUP)], axis=1)
            ms = [jnp.max(s[:, h * nk:(h + 1) * nk], axis=-1, keepdims=True) for h in range(HEADS_PER_GROUP)]
            p_cat = jnp.concatenate([jnp.exp2(s[:, h * nk:(h + 1) * nk] - ms[h])
                                     for h in range(HEADS_PER_GROUP)], axis=1).astype(jnp.bfloat16)
        else:
            q_both = jnp.concatenate([keep_head(q2, h) for h in range(HEADS_PER_GROUP)], axis=0)
            s = lax.dot_general(q_both, k2, contract_cols, preferred_element_type=jnp.float32)
            bias = bias_ref[br, pl.ds(first_head, HEADS_PER_GROUP), :, 2 * WIN - nk:2 * WIN]
            s = s + bias.reshape(HEADS_PER_GROUP * WIN, nk)
            m = jnp.max(s, axis=-1, keepdims=True)
            ms = [m[h * WIN:(h + 1) * WIN] for h in range(HEADS_PER_GROUP)]
            p = jnp.exp2(s - m).astype(jnp.bfloat16)
            p_cat = jnp.concatenate([p[h * WIN:(h + 1) * WIN] for h in range(HEADS_PER_GROUP)], axis=1)
        v_cat = jnp.concatenate(
            [jnp.concatenate([keep_head(v2, h), jnp.broadcast_to(head_ones[h], (nk, LANES))], axis=1)
             for h in range(HEADS_PER_GROUP)], axis=0)
        r = jnp.dot(p_cat, v_cat, preferred_element_type=jnp.float32)
        o_ref[br, rows(dst_start, dst_stride), :] = r[:, :LANES]
        l_ref[br, rows(dst_start, dst_stride), :] = r[:, LANES:]
        m_ref[br, rows(dst_start, dst_stride), :] = jnp.where(head_lanes[0], ms[0], ms[1])

    def rows(start, stride):
        return pl.ds(start, WIN) if stride == 1 else pl.ds(start, WIN, stride=stride)

    gathered = []
    for br, (_, dil) in enumerate(BRANCHES):
        assert dil == 1 or dil % SUB_DIL == 0
        nb = S // dil // WIN
        gathered.append(dil % 8 == 0)
        assert not gathered[br] or nb == 1
        for first in (True, False):
            for res in range(dil):
                for n in range(0 if first else 1, 1 if first else nb):
                    dst = (res * RESIDUE_PITCH, 1) if gathered[br] else (res + dil * WIN * n, dil)
                    window_block(br, dil, res, n, *dst)

    def merged_rows(c):
        def branch_rows(ref, br):
            if not gathered[br]:
                return ref[br, pl.ds(pl.multiple_of(c * MERGE_ROWS, MERGE_ROWS), MERGE_ROWS), :]
            dil = BRANCHES[br][1]
            j0 = c * (MERGE_ROWS // dil)
            return jnp.concatenate(
                [ref[br, pl.ds(j0 + i, dil, stride=RESIDUE_PITCH), :] for i in range(MERGE_ROWS // dil)],
                axis=0)

        ms = [branch_rows(m_ref, br) for br in range(3)]
        m_all = jnp.maximum(jnp.maximum(ms[0], ms[1]), ms[2])
        ws = [jnp.exp2(mb - m_all) for mb in ms]
        num = sum(ws[br] * branch_rows(o_ref, br) for br in range(3))
        den = sum(ws[br] * branch_rows(l_ref, br) for br in range(3))
        return num / den

    return merged_rows


def _attention(qkv, bias_rows):
    B, S, _ = qkv.shape
    n_br = len(BRANCHES)
    qkv_specs = [pl.BlockSpec((1, S, LANES),
                              lambda b, p, part=part, g=g: (b, 0, part * N_GROUPS + p * ATTN_UNITS + g))
                 for g in range(ATTN_UNITS) for part in range(3)]
    stat_rows = max(S, max(d for _, d in BRANCHES) * RESIDUE_PITCH)
    return pl.pallas_call(
        _attn_kernel,
        grid=(B, N_GROUPS // ATTN_UNITS),
        in_specs=qkv_specs + [pl.BlockSpec((n_br, N_HEADS, BIAS_ROW), lambda b, p: (0, 0, 0))],
        out_specs=pl.BlockSpec((1, S, ATTN_UNITS * LANES), lambda b, p: (b, 0, p)),
        out_shape=jax.ShapeDtypeStruct((B, S, D_ATTN), jnp.bfloat16),
        scratch_shapes=[pltpu.VMEM((n_br, N_HEADS, WIN, 2 * WIN), jnp.float32),
                        pltpu.VMEM((ATTN_UNITS, 3, S, LANES), jnp.float32)] +
                       [pltpu.VMEM((ATTN_UNITS, n_br, stat_rows, LANES), jnp.float32)] * 3,
        compiler_params=pltpu.CompilerParams(
            dimension_semantics=("arbitrary", "arbitrary"),
            vmem_limit_bytes=VMEM_LIMIT_BYTES),
        name="dilated_attn",
    )(*([qkv] * (3 * ATTN_UNITS)), bias_rows)


def _t5_causal_bucket(distance):
    max_exact = NUM_BUCKETS // 2
    d = jnp.maximum(distance, 1).astype(jnp.float32)
    large = max_exact + (jnp.log(d / max_exact) / math.log(MAX_DISTANCE / max_exact)
                         * (NUM_BUCKETS - max_exact)).astype(jnp.int32)
    large = jnp.minimum(large, NUM_BUCKETS - 1)
    return jnp.where(distance < max_exact, distance, large)


def _bias_rows(rel_bias):
    dist = 2 * WIN - jnp.arange(BIAS_ROW, dtype=jnp.int32)
    valid = (dist >= 0) & (dist <= WIN)
    rows = []
    for _, dil in BRANCHES:
        bucket = _t5_causal_bucket(jnp.clip(dist, 0, WIN) * dil)
        bias = jnp.zeros((N_HEADS, BIAS_ROW), jnp.float32)
        for b in range(NUM_BUCKETS):
            bias = jnp.where(bucket[None] == b, rel_bias[b].astype(jnp.float32)[:, None], bias)
        rows.append(jnp.where(valid[None], bias * LOG2E, NEG_INF))
    return jnp.stack(rows)


def _ffn_kernel(x_ref, attn_ref, u_ref, uh_ref, cw_ref, cb_ref, lng_ref, lnb_ref,
                wo_ref, g2_ref, w1_ref, w2_ref, gf_ref, out_ref, ubuf_ref, conv_ref, *, tiles_per_seq):
    tm = FFN_ROWS
    step = pl.program_id(0)
    last_step = pl.num_programs(0) - 1
    n_slabs = D_CONV // LANES
    n_ff = D_FF // FF_CHUNK
    n_chunks = tm // CONV_CHUNK
    off = CONV_HALO - (CONV_WIDTH - 1)

    def exact_zero(tile):
        half_word = jnp.uint32(16)
        bits = lax.shift_right_logical(pltpu.bitcast(tile, jnp.uint32), half_word)
        return pltpu.bitcast(lax.shift_right_logical(bits, half_word), jnp.float32)

    def conv_chunk(c):
        base = c * CONV_CHUNK
        slabs = []
        for s in range(n_slabs):
            cols = slice(s * LANES, (s + 1) * LANES)
            a = jnp.zeros((CONV_CHUNK, LANES), jnp.float32) + cb_ref[:, cols]
            for j in range(CONV_WIDTH):
                tap = ubuf_ref[s, pl.ds(base + off + j, CONV_CHUNK, stride=1), :]
                a = a + tap * cw_ref[j:j + 1, cols]
            slabs.append(a)
        y = jnp.concatenate(slabs, axis=-1)
        mu = jnp.mean(y, axis=-1, keepdims=True)
        cen = y - mu
        var = jnp.mean(cen * cen, axis=-1, keepdims=True)
        y = cen * lax.rsqrt(var + LN_EPS) * lng_ref[...] + lnb_ref[...]
        y = y * jax.nn.sigmoid(y)
        conv_ref[base:base + CONV_CHUNK, :] = y.astype(jnp.bfloat16)
        fold = y[0:8, 0:LANES]
        for r in range(CONV_CHUNK // 8):
            for s in range(n_slabs):
                if r or s:
                    fold = jnp.maximum(fold, y[8 * r:8 * r + 8, s * LANES:(s + 1) * LANES])
        return exact_zero(fold)

    def step_body(ffn_half, conv_half):
        if ffn_half:
            mixed = jnp.concatenate([attn_ref[0], conv_ref[...]], axis=-1)
            x1 = x_ref[0] + jnp.dot(mixed, wo_ref[...], preferred_element_type=jnp.float32)
            inv = lax.rsqrt(jnp.mean(x1 * x1, axis=-1, keepdims=True) + RMS_EPS)
            h = (x1 * inv * g2_ref[...]).astype(jnp.bfloat16)
        if conv_half:
            seq_start = (step % tiles_per_seq) == 0
            ubuf_ref[:, 0:CONV_HALO, :] = jnp.where(seq_start, 0.0, uh_ref[0])
            ubuf_ref[:, CONV_HALO:CONV_HALO + tm, :] = u_ref[0]
        if not ffn_half:
            for c in range(n_chunks):
                conv_chunk(c)
            return

        chunk_ids = iter(range(n_chunks))
        anchor_cols = FF_CHUNK * n_ff // n_chunks
        assert anchor_cols % LANES == 0

        def anchored(d):
            top = []
            for g in range(d.shape[1] // anchor_cols):
                top.append(d[0:8, anchor_cols * g:anchor_cols * g + LANES] + conv_chunk(next(chunk_ids)))
                if anchor_cols > LANES:
                    top.append(d[0:8, anchor_cols * g + LANES:anchor_cols * (g + 1)])
            return jnp.concatenate([jnp.concatenate(top, axis=1), d[8:]], axis=0)

        acc = x1
        for c in range(n_ff):
            cols = slice(c * FF_CHUNK, (c + 1) * FF_CHUNK)
            d = jnp.dot(h, w1_ref[:, cols], preferred_element_type=jnp.float32)
            f = jnp.maximum(anchored(d) if conv_half else d, 0.0)
            f = (f * f).astype(jnp.bfloat16)
            acc = acc + jnp.dot(f, w2_ref[cols, :], preferred_element_type=jnp.float32)
        inv = lax.rsqrt(jnp.mean(acc * acc, axis=-1, keepdims=True) + RMS_EPS)
        out_ref[0] = acc * inv * gf_ref[...]

    pl.when(step == 0)(lambda: step_body(False, True))
    pl.when((step > 0) & (step < last_step))(lambda: step_body(True, True))
    pl.when(step == last_step)(lambda: step_body(True, False))


def _ffn(x, attn, u, conv_w, conv_b, ln_g, ln_b, w_o, g2, w1, w2, gf):
    B, S, _ = x.shape
    tm = FFN_ROWS
    tiles_per_seq = S // tm
    n_tiles = B * tiles_per_seq
    halo_blocks = tm // CONV_HALO
    x = x.reshape(n_tiles, tm, D_MODEL)
    attn = attn.reshape(n_tiles, tm, D_ATTN)

    def ffn_tile(s):
        return (jnp.maximum(s - 1, 0), 0, 0)

    def conv_tile(s):
        t = jnp.minimum(s, n_tiles - 1)
        return t // tiles_per_seq, t % tiles_per_seq

    def u_block(s):
        b, i = conv_tile(s)
        return (b, 0, i, 0)

    def halo_block(s):
        b, i = conv_tile(s)
        return (b, 0, jnp.maximum(i * halo_blocks - 1, 0), 0)

    const = lambda s: (0, 0)
    resident = functools.partial(pl.BlockSpec, index_map=const, pipeline_mode=pl.Buffered(1))
    out = pl.pallas_call(
        functools.partial(_ffn_kernel, tiles_per_seq=tiles_per_seq),
        grid=(n_tiles + 1,),
        in_specs=[
            pl.BlockSpec((1, tm, D_MODEL), ffn_tile),
            pl.BlockSpec((1, tm, D_ATTN), ffn_tile),
            pl.BlockSpec((1, D_CONV // LANES, tm, LANES), u_block),
            pl.BlockSpec((1, D_CONV // LANES, CONV_HALO, LANES), halo_block),
            pl.BlockSpec((CONV_WIDTH, D_CONV), const),
            pl.BlockSpec((1, D_CONV), const),
            pl.BlockSpec((1, D_CONV), const),
            pl.BlockSpec((1, D_CONV), const),
            resident((D_MODEL, D_MODEL)),
            pl.BlockSpec((1, D_MODEL), const),
            resident((D_MODEL, D_FF)),
            resident((D_FF, D_MODEL)),
            pl.BlockSpec((1, D_MODEL), const),
        ],
        out_specs=pl.BlockSpec((1, tm, D_MODEL), ffn_tile),
        out_shape=jax.ShapeDtypeStruct((n_tiles, tm, D_MODEL), jnp.float32),
        scratch_shapes=[
            pltpu.VMEM((D_CONV // LANES, CONV_HALO + tm, LANES), jnp.float32),
            pltpu.VMEM((tm, D_CONV), jnp.bfloat16),
        ],
        compiler_params=pltpu.CompilerParams(
            dimension_semantics=("arbitrary",),
            vmem_limit_bytes=VMEM_LIMIT_BYTES),
        name="conv_outproj_ffn",
    )(x, attn, u, u, conv_w, conv_b, ln_g, ln_b, w_o, g2, w1, w2, gf)
    return out.reshape(B, S, D_MODEL)


def kernel(x, norm1_g, w_in, conv_w, conv_b, conv_ln_g, conv_ln_b, w_o, norm2_g, w_ff1, w_ff2,
           rel_bias, final_g):
    assert norm1_g.shape[0] == 1, "single-layer problem"
    qkv, u, w_o_bf, w1_bf, w2_bf = _project(x, norm1_g, w_in[0], w_o[0], w_ff1[0], w_ff2[0])
    attn = _attention(qkv, _bias_rows(rel_bias))
    return _ffn(x, attn, u, conv_w[0], conv_b, conv_ln_g, conv_ln_b, w_o_bf, norm2_g,
                w1_bf, w2_bf, final_g.reshape(1, D_MODEL))
```

```python
import functools
import math

import jax
import jax.numpy as jnp
from jax import lax
from jax.experimental import pallas as pl
from jax.experimental.pallas import tpu as pltpu

D_MODEL = 1024
N_HEADS = 8
HEAD_DIM = 64
D_ATTN = N_HEADS * HEAD_DIM
D_CONV = D_MODEL - D_ATTN
D_QKV = 3 * D_ATTN
D_IN_PROJ = D_QKV + 2 * D_CONV
CONV_WIDTH = 31
D_FF = 4 * D_MODEL
BRANCHES = ((128, 1), (512, 4), (2048, 16))
WIN = 128
NUM_BUCKETS = 32
MAX_DISTANCE = 2048
RMS_EPS = 1e-6
LN_EPS = 1e-5
NEG_INF = -1e30
LOG2E = math.log2(math.e)

LANES = 128
BF16_TILE_ROWS = 16
PROJ_COLS = 256
HEADS_PER_GROUP = LANES // HEAD_DIM
N_GROUPS = N_HEADS // HEADS_PER_GROUP
VMEM_LIMIT_BYTES = 56 * 1024 * 1024

PROJ_ROWS = 1024
CONV_HALO = 32
CONV_CHUNK = 32
SUB_DIL = 4
RESIDUE_PITCH = 136
ATTN_UNITS = 2
MERGE_ROWS = 512
BIAS_ROW = 4 * WIN
FFN_ROWS = 512
FF_CHUNK = 512


def _proj_kernel(x_ref, g1_ref, win_f32_ref, wo_ref, w1_ref, w2_ref,
                 qkv_ref, u_ref, wo_bf_ref, w1_bf_ref, w2_bf_ref, win_ref):
    @pl.when((pl.program_id(0) == 0) & (pl.program_id(1) == 0))
    def _():
        win_ref[...] = win_f32_ref[...].astype(jnp.bfloat16)

    wo_bf_ref[...] = wo_ref[...].astype(jnp.bfloat16)
    w1_bf_ref[...] = w1_ref[...].astype(jnp.bfloat16)
    w2_bf_ref[...] = w2_ref[...].astype(jnp.bfloat16)

    x = x_ref[0]
    inv = lax.rsqrt(jnp.mean(x * x, axis=-1, keepdims=True) + RMS_EPS)
    h = (x * inv * g1_ref[...]).astype(jnp.bfloat16)

    slabs_per_group = PROJ_COLS // LANES
    for g in range(D_CONV // PROJ_COLS):
        a_cols = slice(D_QKV + g * PROJ_COLS, D_QKV + (g + 1) * PROJ_COLS)
        gate_cols = slice(D_QKV + D_CONV + g * PROJ_COLS, D_QKV + D_CONV + (g + 1) * PROJ_COLS)
        a = jnp.dot(h, win_ref[:, a_cols], preferred_element_type=jnp.float32)
        gate = jnp.dot(h, win_ref[:, gate_cols], preferred_element_type=jnp.float32)
        u = a * jax.nn.sigmoid(gate)
        for k in range(slabs_per_group):
            u_ref[0, slabs_per_group * g + k] = u[:, k * LANES:(k + 1) * LANES]

    q_scale = LOG2E / math.sqrt(HEAD_DIM)
    for g in range(D_QKV // PROJ_COLS):
        cols = slice(g * PROJ_COLS, (g + 1) * PROJ_COLS)
        z = jnp.dot(h, win_ref[:, cols], preferred_element_type=jnp.float32)
        if g * PROJ_COLS < D_ATTN:
            z = z * q_scale
        qkv_ref[0, :, cols] = z


def _project(x, g1, w_in, w_o, w1, w2):
    B, S, _ = x.shape
    tm = PROJ_ROWS
    tiles_per_seq = S // tm
    n_steps = B * tiles_per_seq
    const = lambda b, i: (0, 0)
    step_rows = lambda b, i: (b * tiles_per_seq + i, 0)

    def weight_slice(w):
        rows = w.shape[0] // n_steps
        assert rows * n_steps == w.shape[0] and rows % BF16_TILE_ROWS == 0
        return pl.BlockSpec((rows, w.shape[1]), step_rows)

    weights = (w_o, w1, w2)
    return pl.pallas_call(
        _proj_kernel,
        grid=(B, tiles_per_seq),
        in_specs=[
            pl.BlockSpec((1, tm, D_MODEL), lambda b, i: (b, i, 0)),
            pl.BlockSpec((1, D_MODEL), const),
            pl.BlockSpec((D_MODEL, D_IN_PROJ), const, pipeline_mode=pl.Buffered(1)),
        ] + [weight_slice(w) for w in weights],
        out_specs=[
            pl.BlockSpec((1, tm, D_QKV), lambda b, i: (b, i, 0)),
            pl.BlockSpec((1, D_CONV // LANES, tm, LANES), lambda b, i: (b, 0, i, 0)),
        ] + [weight_slice(w) for w in weights],
        out_shape=[
            jax.ShapeDtypeStruct((B, S, D_QKV), jnp.float32),
            jax.ShapeDtypeStruct((B, D_CONV // LANES, S, LANES), jnp.float32),
        ] + [jax.ShapeDtypeStruct(w.shape, jnp.bfloat16) for w in weights],
        scratch_shapes=[pltpu.VMEM((D_MODEL, D_IN_PROJ), jnp.bfloat16)],
        compiler_params=pltpu.CompilerParams(
            dimension_semantics=("arbitrary", "arbitrary"),
            vmem_limit_bytes=VMEM_LIMIT_BYTES),
        name="in_proj",
    )(x, g1, w_in, *weights)


def _attn_kernel(*refs):
    assert HEADS_PER_GROUP == 2
    qkv_refs = refs[:3 * ATTN_UNITS]
    bias_rows_ref, out_ref, bias_ref, sub_all_ref, o_all_ref, l_all_ref, m_all_ref = refs[3 * ATTN_UNITS:]
    S = out_ref.shape[1]

    @pl.when((pl.program_id(0) == 0) & (pl.program_id(1) == 0))
    def _():
        for br in range(len(BRANCHES)):
            for h in range(N_HEADS):
                row = jnp.broadcast_to(bias_rows_ref[br, h:h + 1, :], (WIN, BIAS_ROW))
                table = pltpu.roll(row, BIAS_ROW - WIN, 1, stride=1, stride_axis=0)
                bias_ref[br, h] = table[:, :2 * WIN]

    merged = [_attn_unit_blocks(qkv_refs[3 * g:3 * g + 3], bias_ref,
                                (pl.program_id(1) * ATTN_UNITS + g) * HEADS_PER_GROUP, S,
                                sub_all_ref.at[g], o_all_ref.at[g], l_all_ref.at[g], m_all_ref.at[g])
              for g in range(ATTN_UNITS)]

    def merge(c, carry):
        rows = pl.ds(pl.multiple_of(c * MERGE_ROWS, MERGE_ROWS), MERGE_ROWS)
        for g in range(ATTN_UNITS):
            out_ref[0, rows, g * LANES:(g + 1) * LANES] = merged[g](c).astype(out_ref.dtype)
        return carry

    lax.fori_loop(0, S // MERGE_ROWS, merge, 0)


def _attn_unit_blocks(qkv_refs, bias_ref, first_head, S, sub_ref, o_ref, l_ref, m_ref):
    nat_refs = tuple(r.at[0] for r in qkv_refs)
    sub_len = S // SUB_DIL

    for a in range(3):
        for r in range(SUB_DIL):
            sub_ref[a, r * sub_len:(r + 1) * sub_len, :] = nat_refs[a][pl.ds(r, sub_len, stride=SUB_DIL), :]

    def load_rows(a, dil, res, j0, n):
        if dil == 1:
            x = nat_refs[a][pl.ds(j0, n), :]
        else:
            step = dil // SUB_DIL
            start = (res % SUB_DIL) * sub_len + step * j0 + res // SUB_DIL
            x = sub_ref[a, pl.ds(start, n), :] if step == 1 else sub_ref[a, pl.ds(start, n, stride=step), :]
        return x.astype(jnp.bfloat16)

    lane = lax.broadcasted_iota(jnp.int32, (1, LANES), 1)
    head_lanes = [(lane >= h * HEAD_DIM) & (lane < (h + 1) * HEAD_DIM) for h in range(HEADS_PER_GROUP)]
    head_bits = [jnp.where(hl, jnp.uint32(0xFFFFFFFF), jnp.uint32(0)) for hl in head_lanes]
    head_ones = [jnp.where(hl, 1.0, 0.0).astype(jnp.bfloat16) for hl in head_lanes]

    def keep_head(x, h):
        return pltpu.bitcast(pltpu.bitcast(x, jnp.uint32) & head_bits[h], jnp.bfloat16)

    def window_block(br, dil, res, n, dst_start, dst_stride):
        first = n == 0
        nk = WIN if first else 2 * WIN
        k0 = n * WIN if first else (n - 1) * WIN
        q2 = load_rows(0, dil, res, n * WIN, WIN)
        k2 = load_rows(1, dil, res, k0, nk)
        v2 = load_rows(2, dil, res, k0, nk)
        contract_cols = (((1,), (1,)), ((), ()))
        if first:
            k_both = jnp.concatenate([keep_head(k2, h) for h in range(HEADS_PER_GROUP)], axis=0)
            s = lax.dot_general(q2, k_both, contract_cols, preferred_element_type=jnp.float32)
            s = s + jnp.concatenate([bias_ref[br, first_head + h, :, WIN:2 * WIN]
                                     for h in range(HEADS_PER_GROUP)], axis=1)
            ms = [jnp.max(s[:, h * nk:(h + 1) * nk], axis=-1, keepdims=True) for h in range(HEADS_PER_GROUP)]
            p_cat = jnp.concatenate([jnp.exp2(s[:, h * nk:(h + 1) * nk] - ms[h])
                                     for h in range(HEADS_PER_GROUP)], axis=1).astype(jnp.bfloat16)
        else:
            q_both = jnp.concatenate([keep_head(q2, h) for h in range(HEADS_PER_GROUP)], axis=0)
            s = lax.dot_general(q_both, k2, contract_cols, preferred_element_type=jnp.float32)
            bias = bias_ref[br, pl.ds(first_head, HEADS_PER_GROUP), :, 2 * WIN - nk:2 * WIN]
            s = s + bias.reshape(HEADS_PER_GROUP * WIN, nk)
            m = jnp.max(s, axis=-1, keepdims=True)
            ms = [m[h * WIN:(h + 1) * WIN] for h in range(HEADS_PER_GROUP)]
            p = jnp.exp2(s - m).astype(jnp.bfloat16)
            p_cat = jnp.concatenate([p[h * WIN:(h + 1) * WIN] for h in range(HEADS_PER_GROUP)], axis=1)
        v_cat = jnp.concatenate(
            [jnp.concatenate([keep_head(v2, h), jnp.broadcast_to(head_ones[h], (nk, LANES))], axis=1)
             for h in range(HEADS_PER_GROUP)], axis=0)
        r = jnp.dot(p_cat, v_cat, preferred_element_type=jnp.float32)
        o_ref[br, rows(dst_start, dst_stride), :] = r[:, :LANES]
        l_ref[br, rows(dst_start, dst_stride), :] = r[:, LANES:]
        m_ref[br, rows(dst_start, dst_stride), :] = jnp.where(head_lanes[0], ms[0], ms[1])

    def rows(start, stride):
        return pl.ds(start, WIN) if stride == 1 else pl.ds(start, WIN, stride=stride)

    gathered = []
    for br, (_, dil) in enumerate(BRANCHES):
        assert dil == 1 or dil % SUB_DIL == 0
        nb = S // dil // WIN
        gathered.append(dil % 8 == 0)
        assert not gathered[br] or nb == 1
        for first in (True, False):
            for res in range(dil):
                for n in range(0 if first else 1, 1 if first else nb):
                    dst = (res * RESIDUE_PITCH, 1) if gathered[br] else (res + dil * WIN * n, dil)
                    window_block(br, dil, res, n, *dst)

    def merged_rows(c):
        def branch_rows(ref, br):
            if not gathered[br]:
                return ref[br, pl.ds(pl.multiple_of(c * MERGE_ROWS, MERGE_ROWS), MERGE_ROWS), :]
            dil = BRANCHES[br][1]
            j0 = c * (MERGE_ROWS // dil)
            return jnp.concatenate(
                [ref[br, pl.ds(j0 + i, dil, stride=RESIDUE_PITCH), :] for i in range(MERGE_ROWS // dil)],
                axis=0)

        ms = [branch_rows(m_ref, br) for br in range(3)]
        m_all = jnp.maximum(jnp.maximum(ms[0], ms[1]), ms[2])
        ws = [jnp.exp2(mb - m_all) for mb in ms]
        num = sum(ws[br] * branch_rows(o_ref, br) for br in range(3))
        den = sum(ws[br] * branch_rows(l_ref, br) for br in range(3))
        return num / den

    return merged_rows


def _attention(qkv, bias_rows):
    B, S, _ = qkv.shape
    n_br = len(BRANCHES)
    qkv_specs = [pl.BlockSpec((1, S, LANES),
                              lambda b, p, part=part, g=g: (b, 0, part * N_GROUPS + p * ATTN_UNITS + g))
                 for g in range(ATTN_UNITS) for part in range(3)]
    stat_rows = max(S, max(d for _, d in BRANCHES) * RESIDUE_PITCH)
    return pl.pallas_call(
        _attn_kernel,
        grid=(B, N_GROUPS // ATTN_UNITS),
        in_specs=qkv_specs + [pl.BlockSpec((n_br, N_HEADS, BIAS_ROW), lambda b, p: (0, 0, 0))],
        out_specs=pl.BlockSpec((1, S, ATTN_UNITS * LANES), lambda b, p: (b, 0, p)),
        out_shape=jax.ShapeDtypeStruct((B, S, D_ATTN), jnp.bfloat16),
        scratch_shapes=[pltpu.VMEM((n_br, N_HEADS, WIN, 2 * WIN), jnp.float32),
                        pltpu.VMEM((ATTN_UNITS, 3, S, LANES), jnp.float32)] +
                       [pltpu.VMEM((ATTN_UNITS, n_br, stat_rows, LANES), jnp.float32)] * 3,
        compiler_params=pltpu.CompilerParams(
            dimension_semantics=("arbitrary", "arbitrary"),
            vmem_limit_bytes=VMEM_LIMIT_BYTES),
        name="dilated_attn",
    )(*([qkv] * (3 * ATTN_UNITS)), bias_rows)


def _t5_causal_bucket(distance):
    max_exact = NUM_BUCKETS // 2
    d = jnp.maximum(distance, 1).astype(jnp.float32)
    large = max_exact + (jnp.log(d / max_exact) / math.log(MAX_DISTANCE / max_exact)
                         * (NUM_BUCKETS - max_exact)).astype(jnp.int32)
    large = jnp.minimum(large, NUM_BUCKETS - 1)
    return jnp.where(distance < max_exact, distance, large)


def _bias_rows(rel_bias):
    dist = 2 * WIN - jnp.arange(BIAS_ROW, dtype=jnp.int32)
    valid = (dist >= 0) & (dist <= WIN)
    rows = []
    for _, dil in BRANCHES:
        bucket = _t5_causal_bucket(jnp.clip(dist, 0, WIN) * dil)
        bias = jnp.zeros((N_HEADS, BIAS_ROW), jnp.float32)
        for b in range(NUM_BUCKETS):
            bias = jnp.where(bucket[None] == b, rel_bias[b].astype(jnp.float32)[:, None], bias)
        rows.append(jnp.where(valid[None], bias * LOG2E, NEG_INF))
    return jnp.stack(rows)


def _ffn_kernel(x_ref, attn_ref, u_ref, uh_ref, cw_ref, cb_ref, lng_ref, lnb_ref,
                wo_ref, g2_ref, w1_ref, w2_ref, gf_ref, out_ref, ubuf_ref, conv_ref, *, tiles_per_seq):
    tm = FFN_ROWS
    step = pl.program_id(0)
    last_step = pl.num_programs(0) - 1
    n_slabs = D_CONV // LANES
    n_ff = D_FF // FF_CHUNK
    n_chunks = tm // CONV_CHUNK
    off = CONV_HALO - (CONV_WIDTH - 1)

    def exact_zero(tile):
        half_word = jnp.uint32(16)
        bits = lax.shift_right_logical(pltpu.bitcast(tile, jnp.uint32), half_word)
        return pltpu.bitcast(lax.shift_right_logical(bits, half_word), jnp.float32)

    def conv_chunk(c):
        base = c * CONV_CHUNK
        slabs = []
        for s in range(n_slabs):
            cols = slice(s * LANES, (s + 1) * LANES)
            a = jnp.zeros((CONV_CHUNK, LANES), jnp.float32) + cb_ref[:, cols]
            for j in range(CONV_WIDTH):
                tap = ubuf_ref[s, pl.ds(base + off + j, CONV_CHUNK, stride=1), :]
                a = a + tap * cw_ref[j:j + 1, cols]
            slabs.append(a)
        y = jnp.concatenate(slabs, axis=-1)
        mu = jnp.mean(y, axis=-1, keepdims=True)
        cen = y - mu
        var = jnp.mean(cen * cen, axis=-1, keepdims=True)
        y = cen * lax.rsqrt(var + LN_EPS) * lng_ref[...] + lnb_ref[...]
        y = y * jax.nn.sigmoid(y)
        conv_ref[base:base + CONV_CHUNK, :] = y.astype(jnp.bfloat16)
        fold = y[0:8, 0:LANES]
        for r in range(CONV_CHUNK // 8):
            for s in range(n_slabs):
                if r or s:
                    fold = jnp.maximum(fold, y[8 * r:8 * r + 8, s * LANES:(s + 1) * LANES])
        return exact_zero(fold)

    def step_body(ffn_half, conv_half):
        if ffn_half:
            mixed = jnp.concatenate([attn_ref[0], conv_ref[...]], axis=-1)
            x1 = x_ref[0] + jnp.dot(mixed, wo_ref[...], preferred_element_type=jnp.float32)
            inv = lax.rsqrt(jnp.mean(x1 * x1, axis=-1, keepdims=True) + RMS_EPS)
            h = (x1 * inv * g2_ref[...]).astype(jnp.bfloat16)
        if conv_half:
            seq_start = (step % tiles_per_seq) == 0
            ubuf_ref[:, 0:CONV_HALO, :] = jnp.where(seq_start, 0.0, uh_ref[0])
            ubuf_ref[:, CONV_HALO:CONV_HALO + tm, :] = u_ref[0]
        if not ffn_half:
            for c in range(n_chunks):
                conv_chunk(c)
            return

        chunk_ids = iter(range(n_chunks))
        anchor_cols = FF_CHUNK * n_ff // n_chunks
        assert anchor_cols % LANES == 0

        def anchored(d):
            top = []
            for g in range(d.shape[1] // anchor_cols):
                top.append(d[0:8, anchor_cols * g:anchor_cols * g + LANES] + conv_chunk(next(chunk_ids)))
                if anchor_cols > LANES:
                    top.append(d[0:8, anchor_cols * g + LANES:anchor_cols * (g + 1)])
            return jnp.concatenate([jnp.concatenate(top, axis=1), d[8:]], axis=0)

        out_ref[0] = x1
        for c in range(n_ff):
            cols = slice(c * FF_CHUNK, (c + 1) * FF_CHUNK)
            d = jnp.dot(h, w1_ref[:, cols], preferred_element_type=jnp.float32)
            f = jnp.maximum(anchored(d) if conv_half else d, 0.0)
            f = (f * f).astype(jnp.bfloat16)
            out_ref[0] += jnp.dot(f, w2_ref[cols, :], preferred_element_type=jnp.float32)
        acc = out_ref[0]
        inv = lax.rsqrt(jnp.mean(acc * acc, axis=-1, keepdims=True) + RMS_EPS)
        out_ref[0] = acc * inv * gf_ref[...]

    pl.when(step == 0)(lambda: step_body(False, True))
    pl.when((step > 0) & (step < last_step))(lambda: step_body(True, True))
    pl.when(step == last_step)(lambda: step_body(True, False))


def _ffn(x, attn, u, conv_w, conv_b, ln_g, ln_b, w_o, g2, w1, w2, gf):
    B, S, _ = x.shape
    tm = FFN_ROWS
    tiles_per_seq = S // tm
    n_tiles = B * tiles_per_seq
    halo_blocks = tm // CONV_HALO
    x = x.reshape(n_tiles, tm, D_MODEL)
    attn = attn.reshape(n_tiles, tm, D_ATTN)

    def ffn_tile(s):
        return (jnp.maximum(s - 1, 0), 0, 0)

    def conv_tile(s):
        t = jnp.minimum(s, n_tiles - 1)
        return t // tiles_per_seq, t % tiles_per_seq

    def u_block(s):
        b, i = conv_tile(s)
        return (b, 0, i, 0)

    def halo_block(s):
        b, i = conv_tile(s)
        return (b, 0, jnp.maximum(i * halo_blocks - 1, 0), 0)

    const = lambda s: (0, 0)
    resident = functools.partial(pl.BlockSpec, index_map=const, pipeline_mode=pl.Buffered(1))
    out = pl.pallas_call(
        functools.partial(_ffn_kernel, tiles_per_seq=tiles_per_seq),
        grid=(n_tiles + 1,),
        in_specs=[
            pl.BlockSpec((1, tm, D_MODEL), ffn_tile),
            pl.BlockSpec((1, tm, D_ATTN), ffn_tile),
            pl.BlockSpec((1, D_CONV // LANES, tm, LANES), u_block),
            pl.BlockSpec((1, D_CONV // LANES, CONV_HALO, LANES), halo_block),
            pl.BlockSpec((CONV_WIDTH, D_CONV), const),
            pl.BlockSpec((1, D_CONV), const),
            pl.BlockSpec((1, D_CONV), const),
            pl.BlockSpec((1, D_CONV), const),
            resident((D_MODEL, D_MODEL)),
            pl.BlockSpec((1, D_MODEL), const),
            resident((D_MODEL, D_FF)),
            resident((D_FF, D_MODEL)),
            pl.BlockSpec((1, D_MODEL), const),
        ],
        out_specs=pl.BlockSpec((1, tm, D_MODEL), ffn_tile),
        out_shape=jax.ShapeDtypeStruct((n_tiles, tm, D_MODEL), jnp.float32),
        scratch_shapes=[
            pltpu.VMEM((D_CONV // LANES, CONV_HALO + tm, LANES), jnp.float32),
            pltpu.VMEM((tm, D_CONV), jnp.bfloat16),
        ],
        compiler_params=pltpu.CompilerParams(
            dimension_semantics=("arbitrary",),
            vmem_limit_bytes=VMEM_LIMIT_BYTES),
        name="conv_outproj_ffn",
    )(x, attn, u, u, conv_w, conv_b, ln_g, ln_b, w_o, g2, w1, w2, gf)
    return out.reshape(B, S, D_MODEL)


def kernel(x, norm1_g, w_in, conv_w, conv_b, conv_ln_g, conv_ln_b, w_o, norm2_g, w_ff1, w_ff2,
           rel_bias, final_g):
    assert norm1_g.shape[0] == 1, "single-layer problem"
    qkv, u, w_o_bf, w1_bf, w2_bf = _project(x, norm1_g, w_in[0], w_o[0], w_ff1[0], w_ff2[0])
    attn = _attention(qkv, _bias_rows(rel_bias))
    return _ffn(x, attn, u, conv_w[0], conv_b, conv_ln_g, conv_ln_b, w_o_bf, norm2_g,
                w1_bf, w2_bf, final_g.reshape(1, D_MODEL))
```
